```python
import math
import jax, jax.numpy as jnp
from jax import lax
import numpy as np

D_MODEL = 1024
BATCH = 2
SEQ = 8192
DEPTH = 2

N_MIXERS = 2
N_CONV_LAYERS = (DEPTH + 1) // 2
N_ATTN_LAYERS = DEPTH // 2
D_FF = 2816
CONV_WIDTH = 31
DIFF_HEAD_DIM = 64
N_DIFF_HEADS = D_MODEL // (2 * DIFF_HEAD_DIM)
Q_BLOCK = 128
EPS = 1e-6

kernel_name = "hybrid_conformer_conv_diff_attn_macaron"


def _rmsnorm(x, g):
    xf = x.astype(jnp.float32)
    y = xf * lax.rsqrt(jnp.mean(xf * xf, axis=-1, keepdims=True) + EPS)
    return (y * g.astype(jnp.float32)).astype(x.dtype)


def _layernorm(x, g, b):
    xf = x.astype(jnp.float32)
    mu = jnp.mean(xf, axis=-1, keepdims=True)
    var = jnp.mean(jnp.square(xf - mu), axis=-1, keepdims=True)
    y = (xf - mu) * lax.rsqrt(var + EPS)
    return (y * g.astype(jnp.float32) + b.astype(jnp.float32)).astype(x.dtype)


def _swiglu_ffn(h, w_in, w_out):
    gu = h @ w_in
    gate, up = jnp.split(gu, 2, axis=-1)
    return (jax.nn.silu(gate) * up) @ w_out


def _conformer_conv(h, w_in, b_in, dw, dw_b, ln_g, ln_b, w_out, b_out):
    ab = h @ w_in + b_in
    a, b = jnp.split(ab, 2, axis=-1)
    u = a * jax.nn.sigmoid(b)
    u = lax.conv_general_dilated(
        u, dw[:, None, :].astype(u.dtype), window_strides=(1,),
        padding=[(CONV_WIDTH - 1, 0)],
        dimension_numbers=("NWC", "WIO", "NWC"),
        feature_group_count=D_MODEL) + dw_b
    u = _layernorm(u, ln_g, ln_b)
    u = jax.nn.silu(u)
    return u @ w_out + b_out


def _diff_attention(h, w_qkv, q_norm, k_norm, lq1, lk1, lq2, lk2, subln_g, w_out, layer_idx):
    B, S, _ = h.shape
    H, d = N_DIFF_HEADS, DIFF_HEAD_DIM
    lambda_init = 0.8 - 0.6 * math.exp(-0.3 * layer_idx)
    qkv = h @ w_qkv
    q, k, v = jnp.split(qkv, 3, axis=-1)
    q = _rmsnorm(q.reshape(B, S, H, 2, d), q_norm)
    k = _rmsnorm(k.reshape(B, S, H, 2, d), k_norm)
    v = v.reshape(B, S, H, 2 * d)
    q = jnp.transpose(q, (0, 2, 3, 1, 4)) * (1.0 / math.sqrt(d))
    k = jnp.transpose(k, (0, 2, 3, 1, 4))
    v = jnp.transpose(v, (0, 2, 1, 3))

    lam = (jnp.exp(jnp.sum(lq1.astype(jnp.float32) * lk1.astype(jnp.float32)))
           - jnp.exp(jnp.sum(lq2.astype(jnp.float32) * lk2.astype(jnp.float32)))
           + lambda_init)
    slopes = jnp.exp2(-8.0 * jnp.arange(1, H + 1, dtype=jnp.float32) / H)

    n_blk = S // Q_BLOCK
    q_blocks = jnp.moveaxis(q.reshape(B, H, 2, n_blk, Q_BLOCK, d), 3, 0)
    k_pos = jnp.arange(S)

    def block(args):
        qb, bi = args
        s = jnp.einsum("bhcqd,bhckd->bhcqk", qb, k).astype(jnp.float32)
        q_pos = bi * Q_BLOCK + jnp.arange(Q_BLOCK)
        dist = (q_pos[:, None] - k_pos[None, :]).astype(jnp.float32)
        bias = -slopes[:, None, None] * jnp.abs(dist)[None]
        s = jnp.where((dist >= 0)[None, None, None], s + bias[None, :, None], -jnp.inf)
        p = jax.nn.softmax(s, axis=-1)
        a = p[:, :, 0] - lam * p[:, :, 1]
        return jnp.einsum("bhqk,bhke->bhqe", a.astype(v.dtype), v)

    o = lax.map(block, (q_blocks, jnp.arange(n_blk)))
    o = jnp.transpose(o, (1, 0, 3, 2, 4)).reshape(B, S, H, 2 * d)
    o = _rmsnorm(o, subln_g) * (1.0 - lambda_init)
    return o.reshape(B, S, H * 2 * d) @ w_out


def setup_inputs(seed: int = 0) -> dict:
    key = jax.random.key(seed)
    ks = jax.random.split(key, 24)
    D, F, W, H, d = D_MODEL, D_FF, CONV_WIDTH, N_DIFF_HEADS, DIFF_HEAD_DIM
    NC, NA = N_CONV_LAYERS, N_ATTN_LAYERS
    nrm = lambda k, shape, s: jax.random.normal(k, shape, jnp.float32) * s
    gain = lambda k, shape: 1.0 + nrm(k, shape, 0.02)
    return {
        "x": nrm(ks[0], (BATCH, SEQ, D), 1.0),
        "ffn_norm": gain(ks[1], (DEPTH, 2, D)),
        "ffn_w_in": nrm(ks[2], (DEPTH, 2, D, 2 * F), D ** -0.5),
        "ffn_w_out": nrm(ks[3], (DEPTH, 2, F, D), F ** -0.5),
        "mix_norm": gain(ks[4], (DEPTH, D)),
        "conv_w_in": nrm(ks[5], (NC, D, 2 * D), D ** -0.5),
        "conv_b_in": nrm(ks[6], (NC, 2 * D), 0.02),
        "conv_dw": nrm(ks[7], (NC, W, D), W ** -0.5),
        "conv_dw_b": nrm(ks[8], (NC, D), 0.02),
        "conv_ln_g": gain(ks[9], (NC, D)),
        "conv_ln_b": nrm(ks[10], (NC, D), 0.02),
        "conv_w_out": nrm(ks[11], (NC, D, D), D ** -0.5),
        "conv_b_out": nrm(ks[12], (NC, D), 0.02),
        "attn_w_qkv": nrm(ks[13], (NA, D, 3 * D), D ** -0.5),
        "attn_q_norm": gain(ks[14], (NA, d)),
        "attn_k_norm": gain(ks[15], (NA, d)),
        "attn_lq1": nrm(ks[16], (NA, d), 0.1),
        "attn_lk1": nrm(ks[17], (NA, d), 0.1),
        "attn_lq2": nrm(ks[18], (NA, d), 0.1),
        "attn_lk2": nrm(ks[19], (NA, d), 0.1),
        "attn_subln_g": gain(ks[20], (NA, 2 * d)),
        "attn_w_out": nrm(ks[21], (NA, D, D), D ** -0.5),
    }


def reference(x, ffn_norm, ffn_w_in, ffn_w_out, mix_norm,
              conv_w_in, conv_b_in, conv_dw, conv_dw_b, conv_ln_g, conv_ln_b, conv_w_out, conv_b_out,
              attn_w_qkv, attn_q_norm, attn_k_norm, attn_lq1, attn_lk1, attn_lq2, attn_lk2,
              attn_subln_g, attn_w_out):
    for i in range(DEPTH):
        x = x + 0.5 * _swiglu_ffn(_rmsnorm(x, ffn_norm[i, 0]), ffn_w_in[i, 0], ffn_w_out[i, 0])
        h = _rmsnorm(x, mix_norm[i])
        j = i // N_MIXERS
        if i % N_MIXERS == 0:
            m = _conformer_conv(h, conv_w_in[j], conv_b_in[j], conv_dw[j], conv_dw_b[j],
                                conv_ln_g[j], conv_ln_b[j], conv_w_out[j], conv_b_out[j])
        else:
            m = _diff_attention(h, attn_w_qkv[j], attn_q_norm[j], attn_k_norm[j],
                                attn_lq1[j], attn_lk1[j], attn_lq2[j], attn_lk2[j],
                                attn_subln_g[j], attn_w_out[j], i)
        x = x + m
        x = x + 0.5 * _swiglu_ffn(_rmsnorm(x, ffn_norm[i, 1]), ffn_w_in[i, 1], ffn_w_out[i, 1])
    return x
```

```python
import functools
import math

import jax
import jax.numpy as jnp
from jax import lax
from jax.experimental import pallas as pl
from jax.experimental.pallas import tpu as pltpu

EPS = 1e-6
CONV_WIDTH = 31
HEAD_DIM = 64
HEAD_WIDTH = 2 * HEAD_DIM
LANES = 128
HALO = 32
NEG_BIG = -1e30

TM = 512
TQ = 512
TK = 512
CONV_ROWS = 64
VMEM_LIMIT = 56 * 1024 * 1024

F32 = jnp.float32
BF16 = jnp.bfloat16


def _rms(x, g):
    return x * lax.rsqrt(jnp.mean(x * x, axis=-1, keepdims=True) + EPS) * g


def _resident(shape):
    zeros = (0,) * len(shape)
    return pl.BlockSpec(shape, lambda *_: zeros, pipeline_mode=pl.Buffered(1))


def _ffn_kernel(x_ref, g_ref, wg_ref, wu_ref, wo_ref, o_ref, *, n_chunks):
    x = x_ref[...]
    h = _rms(x, g_ref[...]).astype(BF16)
    fc = wg_ref.shape[1] // n_chunks
    y = jnp.zeros(x.shape, F32)
    for c in range(n_chunks):
        sl = slice(c * fc, (c + 1) * fc)
        gate = jnp.dot(h, wg_ref[:, sl], preferred_element_type=F32)
        up = jnp.dot(h, wu_ref[:, sl], preferred_element_type=F32)
        act = (gate * jax.nn.sigmoid(gate) * up).astype(BF16)
        y = y + jnp.dot(act, wo_ref[sl, :], preferred_element_type=F32)
    o_ref[...] = x + 0.5 * y


def _ffn(x2, g, wg, wu, wo):
    t, d = x2.shape
    f = wg.shape[1]
    return pl.pallas_call(
        functools.partial(_ffn_kernel, n_chunks=2),
        grid=(t // TM,),
        in_specs=[
            pl.BlockSpec((TM, d), lambda i: (i, 0)),
            _resident((1, d)),
            _resident((d, f)),
            _resident((d, f)),
            _resident((f, d)),
        ],
        out_specs=pl.BlockSpec((TM, d), lambda i: (i, 0)),
        out_shape=jax.ShapeDtypeStruct((t, d), F32),
        compiler_params=pltpu.CompilerParams(
            dimension_semantics=("arbitrary",), vmem_limit_bytes=VMEM_LIMIT),
        name="ffn",
    )(x2, g, wg, wu, wo)


def _conv_kernel(x_ref, g_ref, wa_ref, wb_ref, ba_ref, bb_ref, dw_ref, dwb_ref,
                 lng_ref, lnb_ref, wo_ref, bo_ref, o_ref, ubuf, cbuf):
    tm, d = x_ref.shape[1], x_ref.shape[2]
    n_lane_chunks = d // LANES

    @pl.when(pl.program_id(1) == 0)
    def _():
        ubuf[:, 0:HALO, :] = jnp.zeros((n_lane_chunks, HALO, LANES), F32)

    x = x_ref[0]
    h = _rms(x, g_ref[...]).astype(BF16)
    a = jnp.dot(h, wa_ref[...], preferred_element_type=F32) + ba_ref[...]
    b = jnp.dot(h, wb_ref[...], preferred_element_type=F32) + bb_ref[...]
    u = a * jax.nn.sigmoid(b)
    for c in range(n_lane_chunks):
        ubuf[c, HALO:HALO + tm, :] = u[:, c * LANES:(c + 1) * LANES]

    def lane_chunk(c, carry):
        for r0 in range(0, tm, CONV_ROWS):
            acc = jnp.zeros((CONV_ROWS, LANES), F32)
            for k in range(CONV_WIDTH):
                start = HALO + r0 - (CONV_WIDTH - 1) + k
                acc = acc + dw_ref[c, k:k + 1, :] * ubuf[c, start:start + CONV_ROWS, :]
            cbuf[c, r0:r0 + CONV_ROWS, :] = acc
        ubuf[c, 0:HALO, :] = ubuf[c, tm:tm + HALO, :]
        return carry

    lax.fori_loop(0, n_lane_chunks, lane_chunk, 0)

    y = jnp.concatenate([cbuf[c] for c in range(n_lane_chunks)], axis=1) + dwb_ref[...]
    mu = jnp.mean(y, axis=-1, keepdims=True)
    yc = y - mu
    var = jnp.mean(yc * yc, axis=-1, keepdims=True)
    z = yc * lax.rsqrt(var + EPS) * lng_ref[...] + lnb_ref[...]
    z = (z * jax.nn.sigmoid(z)).astype(BF16)
    m = jnp.dot(z, wo_ref[...], preferred_element_type=F32) + bo_ref[...]
    o_ref[0] = x + m


def _conv_mixer(x3, g, wa, wb, ba, bb, dw3, dwb, lng, lnb, wo, bo):
    bsz, s, d = x3.shape
    nl = d // LANES
    kpad = dw3.shape[1]
    return pl.pallas_call(
        _conv_kernel,
        grid=(bsz, s // TM),
        in_specs=[
            pl.BlockSpec((1, TM, d), lambda b, i: (b, i, 0)),
            _resident((1, d)),
            _resident((d, d)), _resident((d, d)),
            _resident((1, d)), _resident((1, d)),
            _resident((nl, kpad, LANES)),
            _resident((1, d)), _resident((1, d)), _resident((1, d)),
            _resident((d, d)), _resident((1, d)),
        ],
        out_specs=pl.BlockSpec((1, TM, d), lambda b, i: (b, i, 0)),
        out_shape=jax.ShapeDtypeStruct((bsz, s, d), F32),
        scratch_shapes=[
            pltpu.VMEM((nl, HALO + TM, LANES), F32),
            pltpu.VMEM((nl, TM, LANES), F32),
        ],
        compiler_params=pltpu.CompilerParams(
            dimension_semantics=("arbitrary", "arbitrary"), vmem_limit_bytes=VMEM_LIMIT),
        name="conv_mixer",
    )(x3, g, wa, wb, ba, bb, dw3, dwb, lng, lnb, wo, bo)


def _group_mean_sq(v, gmat_ref):
    sq = v * v
    hi = sq.astype(BF16)
    lo = (sq - hi.astype(F32)).astype(BF16)
    ssum = (jnp.dot(hi, gmat_ref[...], preferred_element_type=F32)
            + jnp.dot(lo, gmat_ref[...], preferred_element_type=F32))
    return ssum * (1.0 / HEAD_DIM)


def _qkv_kernel(x_ref, g_ref, wq_ref, wk_ref, wv_ref, gq_ref, gk_ref, gmat_ref,
                q1_ref, q2_ref, k_ref, v_ref):
    x = x_ref[...]
    h = _rms(x, g_ref[...]).astype(BF16)
    q = jnp.dot(h, wq_ref[...], preferred_element_type=F32)
    k = jnp.dot(h, wk_ref[...], preferred_element_type=F32)
    v = jnp.dot(h, wv_ref[...], preferred_element_type=F32)
    qn = q * lax.rsqrt(_group_mean_sq(q, gmat_ref) + EPS) * gq_ref[...]
    kn = k * lax.rsqrt(_group_mean_sq(k, gmat_ref) + EPS) * gk_ref[...]
    lane = lax.broadcasted_iota(jnp.int32, qn.shape, 1)
    first = (lane % HEAD_WIDTH) < HEAD_DIM
    q1_ref[...] = jnp.where(first, qn, 0.0).astype(BF16)
    q2_ref[...] = jnp.where(first, 0.0, qn).astype(BF16)
    k_ref[...] = kn.astype(BF16)
    v_ref[...] = v.astype(BF16)


def _qkv(x2, g, wq, wk, wv, gq, gk, gmat):
    t, d = x2.shape
    row = pl.BlockSpec((TM, d), lambda i: (i, 0))
    out = jax.ShapeDtypeStruct((t, d), BF16)
    return pl.pallas_call(
        _qkv_kernel,
        grid=(t // TM,),
        in_specs=[row, _resident((1, d)), _resident((d, d)), _resident((d, d)), _resident((d, d)),
                  _resident((1, d)), _resident((1, d)), _resident((d, d))],
        out_specs=[row, row, row, row],
        out_shape=[out, out, out, out],
        compiler_params=pltpu.CompilerParams(
            dimension_semantics=("arbitrary",), vmem_limit_bytes=VMEM_LIMIT),
        name="qkv_proj",
    )(x2, g, wq, wk, wv, gq, gk, gmat)


def _attn_kernel(slope_ref, q1_ref, q2_ref, k_ref, v_ref, lq1_ref, lk1_ref, lq2_ref, lk2_ref,
                 subg_ref, o_ref, m_sc, l_sc, acc_sc, *, lambda_init):
    hd = pl.program_id(1)
    qi = pl.program_id(2)
    tq = q1_ref.shape[1]
    slope = slope_ref[hd]

    row = lax.broadcasted_iota(jnp.int32, (tq, TK), 0)
    col = lax.broadcasted_iota(jnp.int32, (tq, TK), 1)
    rel = col - row
    brel = slope * rel.astype(F32)

    m_sc[...] = jnp.full(m_sc.shape, NEG_BIG, F32)
    l_sc[...] = jnp.zeros(l_sc.shape, F32)
    acc_sc[...] = jnp.zeros(acc_sc.shape, F32)
    qs = (q1_ref[0], q2_ref[0])

    def tile(j, masked):
        start = pl.multiple_of(j * TK, TK)
        kt = k_ref[0, pl.ds(start, TK), :]
        vt = v_ref[0, pl.ds(start, TK), :]
        shift = slope * ((j - qi) * TK).astype(F32)
        for c in range(2):
            s = lax.dot_general(qs[c], kt, (((1,), (1,)), ((), ())), preferred_element_type=F32)
            t = s + brel
            if masked:
                t = jnp.where(rel <= 0, t, NEG_BIG)
            m_old = m_sc[c]
            m_new = jnp.maximum(m_old, jnp.max(t, axis=1, keepdims=True) + shift)
            alpha = jnp.exp(m_old - m_new)
            p = jnp.exp(t - (m_new - shift))
            l_sc[c] = alpha * l_sc[c] + jnp.sum(p, axis=1, keepdims=True)
            acc_sc[c] = alpha * acc_sc[c] + jnp.dot(p.astype(BF16), vt, preferred_element_type=F32)
            m_sc[c] = m_new

    def body(j, carry):
        tile(j, masked=False)
        return carry

    lax.fori_loop(0, qi, body, 0)
    tile(qi, masked=True)

    lam = (jnp.exp(jnp.sum(lq1_ref[...] * lk1_ref[...], keepdims=True))
           - jnp.exp(jnp.sum(lq2_ref[...] * lk2_ref[...], keepdims=True))
           + lambda_init)
    o = acc_sc[0] / l_sc[0] - lam * (acc_sc[1] / l_sc[1])
    o = _rms(o, subg_ref[...]) * (1.0 - lambda_init)
    o_ref[0] = o.astype(BF16)


def _attention(slopes, q1, q2, k, v, lq1, lk1, lq2, lk2, subg, lambda_init):
    bsz, s, d = q1.shape
    n_heads = d // HEAD_WIDTH
    qspec = pl.BlockSpec((1, TQ, HEAD_WIDTH), lambda b, h, i: (b, i, h))
    kvspec = pl.BlockSpec((1, s, HEAD_WIDTH), lambda b, h, i: (b, 0, h))
    small = lambda n: pl.BlockSpec((1, n), lambda b, h, i: (0, 0))
    return pl.pallas_call(
        functools.partial(_attn_kernel, lambda_init=lambda_init),
        grid=(bsz, n_heads, s // TQ),
        in_specs=[
            pl.BlockSpec(memory_space=pltpu.SMEM),
            qspec, qspec, kvspec, kvspec,
            small(HEAD_DIM), small(HEAD_DIM), small(HEAD_DIM), small(HEAD_DIM),
            small(HEAD_WIDTH),
        ],
        out_specs=qspec,
        out_shape=jax.ShapeDtypeStruct((bsz, s, d), BF16),
        scratch_shapes=[
            pltpu.VMEM((2, TQ, 1), F32),
            pltpu.VMEM((2, TQ, 1), F32),
            pltpu.VMEM((2, TQ, HEAD_WIDTH), F32),
        ],
        compiler_params=pltpu.CompilerParams(
            dimension_semantics=("arbitrary", "arbitrary", "arbitrary"),
            vmem_limit_bytes=VMEM_LIMIT),
        name="diff_attn",
    )(slopes, q1, q2, k, v, lq1, lk1, lq2, lk2, subg)


def _proj_kernel(x_ref, a_ref, w_ref, o_ref):
    o_ref[...] = x_ref[...] + jnp.dot(a_ref[...], w_ref[...], preferred_element_type=F32)


def _out_proj(x2, a2, w):
    t, d = x2.shape
    row = pl.BlockSpec((TM, d), lambda i: (i, 0))
    return pl.pallas_call(
        _proj_kernel,
        grid=(t // TM,),
        in_specs=[row, row, _resident((d, d))],
        out_specs=row,
        out_shape=jax.ShapeDtypeStruct((t, d), F32),
        compiler_params=pltpu.CompilerParams(
            dimension_semantics=("arbitrary",), vmem_limit_bytes=VMEM_LIMIT),
        name="attn_out_proj",
    )(x2, a2, w)


def _row(v):
    return v.reshape(1, -1)


def kernel(x, ffn_norm, ffn_w_in, ffn_w_out, mix_norm, conv_w_in, conv_b_in, conv_dw, conv_dw_b, conv_ln_g, conv_ln_b, conv_w_out, conv_b_out, attn_w_qkv, attn_q_norm, attn_k_norm, attn_lq1, attn_lk1, attn_lq2, attn_lk2, attn_subln_g, attn_w_out):
    bsz, s, d = x.shape
    depth = ffn_norm.shape[0]
    f = ffn_w_out.shape[2]
    n_heads = d // HEAD_WIDTH
    n_mixers = 2

    def ffn(x2, i, j):
        w_in = ffn_w_in[i, j].astype(BF16)
        return _ffn(x2, _row(ffn_norm[i, j]), w_in[:, :f], w_in[:, f:], ffn_w_out[i, j].astype(BF16))

    x2 = x.reshape(bsz * s, d)
    for i in range(depth):
        x2 = ffn(x2, i, 0)
        j = i // n_mixers
        if i % n_mixers == 0:
            w_in = conv_w_in[j].astype(BF16)
            kpad = -(-CONV_WIDTH // 8) * 8
            dw3 = jnp.pad(conv_dw[j], ((0, kpad - CONV_WIDTH), (0, 0)))
            dw3 = dw3.reshape(kpad, d // LANES, LANES).transpose(1, 0, 2)
            x2 = _conv_mixer(
                x2.reshape(bsz, s, d), _row(mix_norm[i]), w_in[:, :d], w_in[:, d:],
                _row(conv_b_in[j][:d]), _row(conv_b_in[j][d:]), dw3, _row(conv_dw_b[j]),
                _row(conv_ln_g[j]), _row(conv_ln_b[j]), conv_w_out[j].astype(BF16),
                _row(conv_b_out[j])).reshape(bsz * s, d)
        else:
            lambda_init = 0.8 - 0.6 * math.exp(-0.3 * i)
            w_qkv = attn_w_qkv[j].astype(BF16)
            gq = _row(jnp.tile(attn_q_norm[j], d // HEAD_DIM)) * (1.0 / math.sqrt(HEAD_DIM))
            gk = _row(jnp.tile(attn_k_norm[j], d // HEAD_DIM))
            group = jnp.arange(d) // HEAD_DIM
            gmat = (group[:, None] == group[None, :]).astype(BF16)
            q1, q2, k, v = _qkv(x2, _row(mix_norm[i]), w_qkv[:, :d], w_qkv[:, d:2 * d], w_qkv[:, 2 * d:],
                                gq, gk, gmat)
            slopes = jnp.exp2(-8.0 * jnp.arange(1, n_heads + 1, dtype=F32) / n_heads)
            to3 = lambda a: a.reshape(bsz, s, d)
            o = _attention(slopes, to3(q1), to3(q2), to3(k), to3(v),
                           _row(attn_lq1[j]), _row(attn_lk1[j]), _row(attn_lq2[j]), _row(attn_lk2[j]),
                           _row(attn_subln_g[j]), lambda_init)
            x2 = _out_proj(x2, o.reshape(bsz * s, d), attn_w_out[j].astype(BF16))
        x2 = ffn(x2, i, 1)
    return x2.reshape(bsz, s, d)
```

```python
import functools
import math

import jax
import jax.numpy as jnp
from jax import lax
from jax.experimental import pallas as pl
from jax.experimental.pallas import tpu as pltpu

EPS = 1e-6
CONV_WIDTH = 31
HEAD_DIM = 64
HEAD_WIDTH = 2 * HEAD_DIM
LANES = 128
HALO = 32
NEG_BIG = -1e30
LOG2E = 1.4426950408889634

TM = 512
TQ = 512
TK = 512
CONV_ROWS = 64
VMEM_LIMIT = 56 * 1024 * 1024

F32 = jnp.float32
BF16 = jnp.bfloat16


def _rms(x, g):
    return x * lax.rsqrt(jnp.mean(x * x, axis=-1, keepdims=True) + EPS) * g


def _resident(shape):
    zeros = (0,) * len(shape)
    return pl.BlockSpec(shape, lambda *_: zeros, pipeline_mode=pl.Buffered(1))


def _ffn_kernel(x_ref, g_ref, wg_ref, wu_ref, wo_ref, o_ref, *, n_chunks):
    x = x_ref[...]
    h = _rms(x, g_ref[...]).astype(BF16)
    fc = wg_ref.shape[1] // n_chunks
    y = jnp.zeros(x.shape, F32)
    for c in range(n_chunks):
        sl = slice(c * fc, (c + 1) * fc)
        gate = jnp.dot(h, wg_ref[:, sl], preferred_element_type=F32)
        up = jnp.dot(h, wu_ref[:, sl], preferred_element_type=F32)
        act = (gate * jax.nn.sigmoid(gate) * up).astype(BF16)
        y = y + jnp.dot(act, wo_ref[sl, :], preferred_element_type=F32)
    o_ref[...] = x + 0.5 * y


def _ffn(x2, g, wg, wu, wo):
    t, d = x2.shape
    f = wg.shape[1]
    return pl.pallas_call(
        functools.partial(_ffn_kernel, n_chunks=2),
        grid=(t // TM,),
        in_specs=[
            pl.BlockSpec((TM, d), lambda i: (i, 0)),
            _resident((1, d)),
            _resident((d, f)),
            _resident((d, f)),
            _resident((f, d)),
        ],
        out_specs=pl.BlockSpec((TM, d), lambda i: (i, 0)),
        out_shape=jax.ShapeDtypeStruct((t, d), F32),
        compiler_params=pltpu.CompilerParams(
            dimension_semantics=("arbitrary",), vmem_limit_bytes=VMEM_LIMIT),
        name="ffn",
    )(x2, g, wg, wu, wo)


def _conv_kernel(x_ref, g_ref, wa_ref, wb_ref, ba_ref, bb_ref, dw_ref, dwb_ref,
                 lng_ref, lnb_ref, wo_ref, bo_ref, o_ref, ubuf, cbuf):
    tm, d = x_ref.shape[1], x_ref.shape[2]
    n_lane_chunks = d // LANES

    @pl.when(pl.program_id(1) == 0)
    def _():
        ubuf[:, 0:HALO, :] = jnp.zeros((n_lane_chunks, HALO, LANES), F32)

    x = x_ref[0]
    h = _rms(x, g_ref[...]).astype(BF16)
    a = jnp.dot(h, wa_ref[...], preferred_element_type=F32) + ba_ref[...]
    b = jnp.dot(h, wb_ref[...], preferred_element_type=F32) + bb_ref[...]
    u = a * jax.nn.sigmoid(b)
    for c in range(n_lane_chunks):
        ubuf[c, HALO:HALO + tm, :] = u[:, c * LANES:(c + 1) * LANES]

    def lane_chunk(c, carry):
        for r0 in range(0, tm, CONV_ROWS):
            acc = jnp.zeros((CONV_ROWS, LANES), F32)
            for k in range(CONV_WIDTH):
                start = HALO + r0 - (CONV_WIDTH - 1) + k
                acc = acc + dw_ref[c, k:k + 1, :] * ubuf[c, start:start + CONV_ROWS, :]
            cbuf[c, r0:r0 + CONV_ROWS, :] = acc
        ubuf[c, 0:HALO, :] = ubuf[c, tm:tm + HALO, :]
        return carry

    lax.fori_loop(0, n_lane_chunks, lane_chunk, 0)

    y = jnp.concatenate([cbuf[c] for c in range(n_lane_chunks)], axis=1) + dwb_ref[...]
    mu = jnp.mean(y, axis=-1, keepdims=True)
    yc = y - mu
    var = jnp.mean(yc * yc, axis=-1, keepdims=True)
    z = yc * lax.rsqrt(var + EPS) * lng_ref[...] + lnb_ref[...]
    z = (z * jax.nn.sigmoid(z)).astype(BF16)
    m = jnp.dot(z, wo_ref[...], preferred_element_type=F32) + bo_ref[...]
    o_ref[0] = x + m


def _conv_mixer(x3, g, wa, wb, ba, bb, dw3, dwb, lng, lnb, wo, bo):
    bsz, s, d = x3.shape
    nl = d // LANES
    kpad = dw3.shape[1]
    return pl.pallas_call(
        _conv_kernel,
        grid=(bsz, s // TM),
        in_specs=[
            pl.BlockSpec((1, TM, d), lambda b, i: (b, i, 0)),
            _resident((1, d)),
            _resident((d, d)), _resident((d, d)),
            _resident((1, d)), _resident((1, d)),
            _resident((nl, kpad, LANES)),
            _resident((1, d)), _resident((1, d)), _resident((1, d)),
            _resident((d, d)), _resident((1, d)),
        ],
        out_specs=pl.BlockSpec((1, TM, d), lambda b, i: (b, i, 0)),
        out_shape=jax.ShapeDtypeStruct((bsz, s, d), F32),
        scratch_shapes=[
            pltpu.VMEM((nl, HALO + TM, LANES), F32),
            pltpu.VMEM((nl, TM, LANES), F32),
        ],
        compiler_params=pltpu.CompilerParams(
            dimension_semantics=("arbitrary", "arbitrary"), vmem_limit_bytes=VMEM_LIMIT),
        name="conv_mixer",
    )(x3, g, wa, wb, ba, bb, dw3, dwb, lng, lnb, wo, bo)


def _group_mean_sq(v, gmat_ref):
    sq = v * v
    hi = sq.astype(BF16)
    lo = (sq - hi.astype(F32)).astype(BF16)
    ssum = (jnp.dot(hi, gmat_ref[...], preferred_element_type=F32)
            + jnp.dot(lo, gmat_ref[...], preferred_element_type=F32))
    return ssum * (1.0 / HEAD_DIM)


def _qkv_kernel(x_ref, g_ref, wq_ref, wk_ref, wv_ref, gq_ref, gk_ref, gmat_ref,
                q1_ref, q2_ref, k_ref, v_ref):
    n_heads, tm = q1_ref.shape[1], q1_ref.shape[4]
    x = x_ref[...]
    h = _rms(x, g_ref[...]).astype(BF16)
    q = jnp.dot(h, wq_ref[...], preferred_element_type=F32)
    k = jnp.dot(h, wk_ref[...], preferred_element_type=F32)
    v = jnp.dot(h, wv_ref[...], preferred_element_type=F32)
    qn = q * lax.rsqrt(_group_mean_sq(q, gmat_ref) + EPS) * gq_ref[...]
    kn = k * lax.rsqrt(_group_mean_sq(k, gmat_ref) + EPS) * gk_ref[...]
    qt = qn.T.reshape(n_heads, HEAD_WIDTH, tm)
    first = lax.broadcasted_iota(jnp.int32, qt.shape, 1) < HEAD_DIM
    q1_ref[0, :, 0] = jnp.where(first, qt, 0.0).astype(BF16)
    q2_ref[0, :, 0] = jnp.where(first, 0.0, qt).astype(BF16)
    k_ref[...] = kn.astype(BF16)
    v_ref[0, :, 0] = v.T.reshape(n_heads, HEAD_WIDTH, tm).astype(BF16)


def _qkv(x2, g, wq, wk, wv, gq, gk, gmat, bsz):
    t, d = x2.shape
    n_heads = d // HEAD_WIDTH
    n_tiles = t // bsz // TM
    row = pl.BlockSpec((TM, d), lambda i: (i, 0))
    slab = pl.BlockSpec((1, n_heads, 1, HEAD_WIDTH, TM), lambda i: (i // n_tiles, 0, i % n_tiles, 0, 0))
    slab_shape = jax.ShapeDtypeStruct((bsz, n_heads, n_tiles, HEAD_WIDTH, TM), BF16)
    return pl.pallas_call(
        _qkv_kernel,
        grid=(t // TM,),
        in_specs=[row, _resident((1, d)), _resident((d, d)), _resident((d, d)), _resident((d, d)),
                  _resident((1, d)), _resident((1, d)), _resident((d, d))],
        out_specs=[slab, slab, row, slab],
        out_shape=[slab_shape, slab_shape, jax.ShapeDtypeStruct((t, d), BF16), slab_shape],
        compiler_params=pltpu.CompilerParams(
            dimension_semantics=("arbitrary",), vmem_limit_bytes=VMEM_LIMIT),
        name="qkv_proj",
    )(x2, g, wq, wk, wv, gq, gk, gmat)


def _attn_kernel(slope_ref, q1_ref, q2_ref, k_ref, v_ref, lq1_ref, lk1_ref, lq2_ref, lk2_ref,
                 subg_ref, o_ref, m_sc, l_sc, acc_sc, *, lambda_init):
    hd = pl.program_id(1)
    qi = pl.program_id(2)
    slope = slope_ref[hd] * LOG2E

    krow = lax.broadcasted_iota(jnp.int32, (TK, TQ), 0)
    qcol = lax.broadcasted_iota(jnp.int32, (TK, TQ), 1)
    kbias = slope * krow.astype(F32)

    m_sc[...] = jnp.full(m_sc.shape, NEG_BIG, F32)
    l_sc[...] = jnp.zeros(l_sc.shape, F32)
    acc_sc[...] = jnp.zeros(acc_sc.shape, F32)
    qs = (q1_ref[0, 0, 0], q2_ref[0, 0, 0])

    def tile(j, masked):
        kt = k_ref[0, pl.ds(pl.multiple_of(j * TK, TK), TK), :]
        vt = v_ref[0, 0, j]
        shift = slope * ((j - qi) * TK).astype(F32)
        for c in range(2):
            z = jnp.dot(kt, qs[c], preferred_element_type=F32) + kbias
            if masked:
                z = jnp.where(krow <= qcol, z, NEG_BIG)
            m_old = m_sc[c]
            m_new = jnp.maximum(m_old, jnp.max(z, axis=0, keepdims=True) + shift)
            alpha = jnp.exp2(m_old - m_new)
            p = jnp.exp2(z - (m_new - shift))
            l_sc[c] = alpha * l_sc[c] + jnp.sum(p, axis=0, keepdims=True)
            acc_sc[c] = alpha * acc_sc[c] + jnp.dot(vt, p.astype(BF16), preferred_element_type=F32)
            m_sc[c] = m_new

    def body(j, carry):
        tile(j, masked=False)
        return carry

    lax.fori_loop(0, qi, body, 0)
    tile(qi, masked=True)

    lam = (jnp.exp(jnp.sum(lq1_ref[...] * lk1_ref[...], keepdims=True))
           - jnp.exp(jnp.sum(lq2_ref[...] * lk2_ref[...], keepdims=True))
           + lambda_init)
    ot = acc_sc[0] / l_sc[0] - lam * (acc_sc[1] / l_sc[1])
    ot = ot * lax.rsqrt(jnp.mean(ot * ot, axis=0, keepdims=True) + EPS) * subg_ref[...]
    o_ref[0] = (ot * (1.0 - lambda_init)).T.astype(BF16)


def _attention(slopes, q1, q2, k, v, lq1, lk1, lq2, lk2, subg, lambda_init):
    bsz, n_heads, n_tiles = q1.shape[:3]
    s, d = k.shape[1], k.shape[2]
    qspec = pl.BlockSpec((1, 1, 1, HEAD_WIDTH, TQ), lambda b, h, i: (b, h, i, 0, 0))
    kspec = pl.BlockSpec((1, s, HEAD_WIDTH), lambda b, h, i: (b, 0, h))
    vspec = pl.BlockSpec((1, 1, n_tiles, HEAD_WIDTH, TK), lambda b, h, i: (b, h, 0, 0, 0))
    small = lambda n: pl.BlockSpec((1, n), lambda b, h, i: (0, 0))
    return pl.pallas_call(
        functools.partial(_attn_kernel, lambda_init=lambda_init),
        grid=(bsz, n_heads, s // TQ),
        in_specs=[
            pl.BlockSpec(memory_space=pltpu.SMEM),
            qspec, qspec, kspec, vspec,
            small(HEAD_DIM), small(HEAD_DIM), small(HEAD_DIM), small(HEAD_DIM),
            pl.BlockSpec((HEAD_WIDTH, 1), lambda b, h, i: (0, 0)),
        ],
        out_specs=pl.BlockSpec((1, TQ, HEAD_WIDTH), lambda b, h, i: (b, i, h)),
        out_shape=jax.ShapeDtypeStruct((bsz, s, d), BF16),
        scratch_shapes=[
            pltpu.VMEM((2, 1, TQ), F32),
            pltpu.VMEM((2, 1, TQ), F32),
            pltpu.VMEM((2, HEAD_WIDTH, TQ), F32),
        ],
        compiler_params=pltpu.CompilerParams(
            dimension_semantics=("arbitrary", "arbitrary", "arbitrary"),
            vmem_limit_bytes=VMEM_LIMIT),
        name="diff_attn",
    )(slopes, q1, q2, k, v, lq1, lk1, lq2, lk2, subg)


def _proj_kernel(x_ref, a_ref, w_ref, o_ref):
    o_ref[...] = x_ref[...] + jnp.dot(a_ref[...], w_ref[...], preferred_element_type=F32)


def _out_proj(x2, a2, w):
    t, d = x2.shape
    row = pl.BlockSpec((TM, d), lambda i: (i, 0))
    return pl.pallas_call(
        _proj_kernel,
        grid=(t // TM,),
        in_specs=[row, row, _resident((d, d))],
        out_specs=row,
        out_shape=jax.ShapeDtypeStruct((t, d), F32),
        compiler_params=pltpu.CompilerParams(
            dimension_semantics=("arbitrary",), vmem_limit_bytes=VMEM_LIMIT),
        name="attn_out_proj",
    )(x2, a2, w)


def _row(v):
    return v.reshape(1, -1)


def kernel(x, ffn_norm, ffn_w_in, ffn_w_out, mix_norm, conv_w_in, conv_b_in, conv_dw, conv_dw_b, conv_ln_g, conv_ln_b, conv_w_out, conv_b_out, attn_w_qkv, attn_q_norm, attn_k_norm, attn_lq1, attn_lk1, attn_lq2, attn_lk2, attn_subln_g, attn_w_out):
    bsz, s, d = x.shape
    depth = ffn_norm.shape[0]
    f = ffn_w_out.shape[2]
    n_heads = d // HEAD_WIDTH
    n_mixers = 2

    def ffn(x2, i, j):
        w_in = ffn_w_in[i, j].astype(BF16)
        return _ffn(x2, _row(ffn_norm[i, j]), w_in[:, :f], w_in[:, f:], ffn_w_out[i, j].astype(BF16))

    x2 = x.reshape(bsz * s, d)
    for i in range(depth):
        x2 = ffn(x2, i, 0)
        j = i // n_mixers
        if i % n_mixers == 0:
            w_in = conv_w_in[j].astype(BF16)
            kpad = -(-CONV_WIDTH // 8) * 8
            dw3 = jnp.pad(conv_dw[j], ((0, kpad - CONV_WIDTH), (0, 0)))
            dw3 = dw3.reshape(kpad, d // LANES, LANES).transpose(1, 0, 2)
            x2 = _conv_mixer(
                x2.reshape(bsz, s, d), _row(mix_norm[i]), w_in[:, :d], w_in[:, d:],
                _row(conv_b_in[j][:d]), _row(conv_b_in[j][d:]), dw3, _row(conv_dw_b[j]),
                _row(conv_ln_g[j]), _row(conv_ln_b[j]), conv_w_out[j].astype(BF16),
                _row(conv_b_out[j])).reshape(bsz * s, d)
        else:
            lambda_init = 0.8 - 0.6 * math.exp(-0.3 * i)
            w_qkv = attn_w_qkv[j].astype(BF16)
            gq = _row(jnp.tile(attn_q_norm[j], d // HEAD_DIM)) * (LOG2E / math.sqrt(HEAD_DIM))
            gk = _row(jnp.tile(attn_k_norm[j], d // HEAD_DIM))
            group = jnp.arange(d) // HEAD_DIM
            gmat = (group[:, None] == group[None, :]).astype(BF16)
            q1, q2, k, v = _qkv(x2, _row(mix_norm[i]), w_qkv[:, :d], w_qkv[:, d:2 * d], w_qkv[:, 2 * d:],
                                gq, gk, gmat, bsz)
            slopes = jnp.exp2(-8.0 * jnp.arange(1, n_heads + 1, dtype=F32) / n_heads)
            o = _attention(slopes, q1, q2, k.reshape(bsz, s, d), v,
                           _row(attn_lq1[j]), _row(attn_lk1[j]), _row(attn_lq2[j]), _row(attn_lk2[j]),
                           attn_subln_g[j].reshape(-1, 1), lambda_init)
            x2 = _out_proj(x2, o.reshape(bsz * s, d), attn_w_out[j].astype(BF16))
        x2 = ffn(x2, i, 1)
    return x2.reshape(bsz, s, d)
```

```python
import functools
import math

import jax
import jax.numpy as jnp
import numpy as np
from jax import lax
from jax.experimental import pallas as pl
from jax.experimental.pallas import tpu as pltpu

EPS = 1e-6
CONV_WIDTH = 31
HEAD_DIM = 64
HEAD_WIDTH = 2 * HEAD_DIM
LANES = 128
MXU_COLS = 256
BF16_SUBLANES = 16
V_ROWS = HEAD_WIDTH + BF16_SUBLANES
N_POS_PIECES = 3
HALO = 32
NEG_BIG = -1e30
LOG2E = 1.4426950408889634

TM = 512
TQ = 512
TK = 512
assert TM == TK == TQ
CONV_ROWS = 64
VMEM_LIMIT = 56 * 1024 * 1024

F32 = jnp.float32
BF16 = jnp.bfloat16


def _rms(x, g):
    return x * lax.rsqrt(jnp.mean(x * x, axis=-1, keepdims=True) + EPS) * g


def _resident(shape):
    zeros = (0,) * len(shape)
    return pl.BlockSpec(shape, lambda *_: zeros, pipeline_mode=pl.Buffered(1))


def _ffn_chunks(f, n_chunks):
    tiles = f // MXU_COLS
    assert tiles * MXU_COLS == f
    edges = [MXU_COLS * (tiles * c // n_chunks) for c in range(n_chunks + 1)]
    return [slice(lo, hi) for lo, hi in zip(edges[:-1], edges[1:])]


def _ffn_kernel(x_ref, g_ref, wg_ref, wu_ref, wo_ref, o_ref, *, n_chunks):
    x = x_ref[...]
    h = _rms(x, g_ref[...]).astype(BF16)
    y = jnp.zeros(x.shape, F32)
    for sl in _ffn_chunks(wg_ref.shape[1], n_chunks):
        gate = jnp.dot(h, wg_ref[:, sl], preferred_element_type=F32)
        up = jnp.dot(h, wu_ref[:, sl], preferred_element_type=F32)
        act = (gate * jax.nn.sigmoid(gate) * up).astype(BF16)
        y = y + jnp.dot(act, wo_ref[sl, :], preferred_element_type=F32)
    o_ref[...] = x + 0.5 * y


def _ffn(x2, g, wg, wu, wo):
    t, d = x2.shape
    f = wg.shape[1]
    return pl.pallas_call(
        functools.partial(_ffn_kernel, n_chunks=2),
        grid=(t // TM,),
        in_specs=[
            pl.BlockSpec((TM, d), lambda i: (i, 0)),
            _resident((1, d)),
            _resident((d, f)),
            _resident((d, f)),
            _resident((f, d)),
        ],
        out_specs=pl.BlockSpec((TM, d), lambda i: (i, 0)),
        out_shape=jax.ShapeDtypeStruct((t, d), F32),
        compiler_params=pltpu.CompilerParams(
            dimension_semantics=("arbitrary",), vmem_limit_bytes=VMEM_LIMIT),
        name="ffn",
    )(x2, g, wg, wu, wo)


def _conv_kernel(x_ref, g_ref, wa_ref, wb_ref, ba_ref, bb_ref, dw_ref, dwb_ref,
                 lng_ref, lnb_ref, wo_ref, bo_ref, o_ref, ubuf, cbuf):
    tm, d = x_ref.shape[1], x_ref.shape[2]
    n_lane_chunks = d // LANES

    @pl.when(pl.program_id(1) == 0)
    def _():
        ubuf[:, 0:HALO, :] = jnp.zeros((n_lane_chunks, HALO, LANES), F32)

    x = x_ref[0]
    h = _rms(x, g_ref[...]).astype(BF16)
    a = jnp.dot(h, wa_ref[...], preferred_element_type=F32) + ba_ref[...]
    b = jnp.dot(h, wb_ref[...], preferred_element_type=F32) + bb_ref[...]
    u = a * jax.nn.sigmoid(b)
    for c in range(n_lane_chunks):
        ubuf[c, HALO:HALO + tm, :] = u[:, c * LANES:(c + 1) * LANES]

    def lane_chunk(c, carry):
        for r0 in range(0, tm, CONV_ROWS):
            acc = jnp.zeros((CONV_ROWS, LANES), F32)
            for k in range(CONV_WIDTH):
                start = HALO + r0 - (CONV_WIDTH - 1) + k
                acc = acc + dw_ref[c, k:k + 1, :] * ubuf[c, start:start + CONV_ROWS, :]
            cbuf[c, r0:r0 + CONV_ROWS, :] = acc
        ubuf[c, 0:HALO, :] = ubuf[c, tm:tm + HALO, :]
        return carry

    lax.fori_loop(0, n_lane_chunks, lane_chunk, 0)

    y = jnp.concatenate([cbuf[c] for c in range(n_lane_chunks)], axis=1) + dwb_ref[...]
    mu = jnp.mean(y, axis=-1, keepdims=True)
    yc = y - mu
    var = jnp.mean(yc * yc, axis=-1, keepdims=True)
    z = yc * lax.rsqrt(var + EPS) * lng_ref[...] + lnb_ref[...]
    z = (z * jax.nn.sigmoid(z)).astype(BF16)
    m = jnp.dot(z, wo_ref[...], preferred_element_type=F32) + bo_ref[...]
    o_ref[0] = x + m


def _conv_mixer(x3, g, wa, wb, ba, bb, dw3, dwb, lng, lnb, wo, bo):
    bsz, s, d = x3.shape
    nl = d // LANES
    kpad = dw3.shape[1]
    return pl.pallas_call(
        _conv_kernel,
        grid=(bsz, s // TM),
        in_specs=[
            pl.BlockSpec((1, TM, d), lambda b, i: (b, i, 0)),
            _resident((1, d)),
            _resident((d, d)), _resident((d, d)),
            _resident((1, d)), _resident((1, d)),
            _resident((nl, kpad, LANES)),
            _resident((1, d)), _resident((1, d)), _resident((1, d)),
            _resident((d, d)), _resident((1, d)),
        ],
        out_specs=pl.BlockSpec((1, TM, d), lambda b, i: (b, i, 0)),
        out_shape=jax.ShapeDtypeStruct((bsz, s, d), F32),
        scratch_shapes=[
            pltpu.VMEM((nl, HALO + TM, LANES), F32),
            pltpu.VMEM((nl, TM, LANES), F32),
        ],
        compiler_params=pltpu.CompilerParams(
            dimension_semantics=("arbitrary", "arbitrary"), vmem_limit_bytes=VMEM_LIMIT),
        name="conv_mixer",
    )(x3, g, wa, wb, ba, bb, dw3, dwb, lng, lnb, wo, bo)


def _group_mean_sq(v, gmat_ref):
    sq = v * v
    hi = sq.astype(BF16)
    lo = (sq - hi.astype(F32)).astype(BF16)
    ssum = (jnp.dot(hi, gmat_ref[...], preferred_element_type=F32)
            + jnp.dot(lo, gmat_ref[...], preferred_element_type=F32))
    return ssum * (1.0 / HEAD_DIM)


def _qkv_kernel(x_ref, g_ref, wq_ref, wk_ref, wv_ref, gq_ref, gk_ref, gmat_ref, kext_ref, qext_ref,
                q1_ref, q2_ref, k1_ref, k2_ref, v_ref):
    n_heads, tm = q1_ref.shape[1], q1_ref.shape[4]
    x = x_ref[...]
    h = _rms(x, g_ref[...]).astype(BF16)
    q = jnp.dot(h, wq_ref[...], preferred_element_type=F32)
    k = jnp.dot(h, wk_ref[...], preferred_element_type=F32)
    v = jnp.dot(h, wv_ref[...], preferred_element_type=F32)
    qn = q * lax.rsqrt(_group_mean_sq(q, gmat_ref) + EPS) * gq_ref[...]
    kn = k * lax.rsqrt(_group_mean_sq(k, gmat_ref) + EPS) * gk_ref[...]
    lane_first = (lax.broadcasted_iota(jnp.int32, kn.shape, 1) % HEAD_WIDTH) < HEAD_DIM
    kext = kext_ref[...]
    k1_ref[...] = jnp.where(lane_first, kn, kext).astype(BF16)
    k2_ref[...] = jnp.where(lane_first, kext, kn).astype(BF16)
    qt = qn.T.reshape(n_heads, HEAD_WIDTH, tm)
    row_first = lax.broadcasted_iota(jnp.int32, qt.shape, 1) < HEAD_DIM
    qext = jnp.broadcast_to(qext_ref[...].reshape(n_heads, HEAD_WIDTH, 1), qt.shape)
    q1_ref[0, :, 0] = jnp.where(row_first, qt, qext).astype(BF16)
    q2_ref[0, :, 0] = jnp.where(row_first, qext, qt).astype(BF16)
    v_ref[0, :, 0, 0:HEAD_WIDTH, :] = v.T.reshape(n_heads, HEAD_WIDTH, tm).astype(BF16)
    ones_row = lax.broadcasted_iota(jnp.int32, (n_heads, V_ROWS - HEAD_WIDTH, tm), 1) == 0
    v_ref[0, :, 0, HEAD_WIDTH:V_ROWS, :] = jnp.where(ones_row, 1.0, 0.0).astype(BF16)


def _qkv(x2, g, wq, wk, wv, gq, gk, gmat, kext, qext, bsz):
    t, d = x2.shape
    n_heads = d // HEAD_WIDTH
    n_tiles = t // bsz // TM
    row = pl.BlockSpec((TM, d), lambda i: (i, 0))
    slab = lambda rows: pl.BlockSpec((1, n_heads, 1, rows, TM), lambda i: (i // n_tiles, 0, i % n_tiles, 0, 0))
    slab_shape = lambda rows: jax.ShapeDtypeStruct((bsz, n_heads, n_tiles, rows, TM), BF16)
    rows_shape = jax.ShapeDtypeStruct((t, d), BF16)
    return pl.pallas_call(
        _qkv_kernel,
        grid=(t // TM,),
        in_specs=[row, _resident((1, d)), _resident((d, d)), _resident((d, d)), _resident((d, d)),
                  _resident((1, d)), _resident((1, d)), _resident((d, d)),
                  _resident((TM, d)), _resident((d, 1))],
        out_specs=[slab(HEAD_WIDTH), slab(HEAD_WIDTH), row, row, slab(V_ROWS)],
        out_shape=[slab_shape(HEAD_WIDTH), slab_shape(HEAD_WIDTH), rows_shape, rows_shape, slab_shape(V_ROWS)],
        compiler_params=pltpu.CompilerParams(
            dimension_semantics=("arbitrary",), vmem_limit_bytes=VMEM_LIMIT),
        name="qkv_proj",
    )(x2, g, wq, wk, wv, gq, gk, gmat, kext, qext)


def _attn_kernel(slope_ref, q1_ref, q2_ref, k1_ref, k2_ref, v_ref, lq1_ref, lk1_ref, lq2_ref, lk2_ref,
                 subg_ref, o_ref, z_sc, p_sc, m_sc, a_sc, acc_sc, *, lambda_init):
    hd = pl.program_id(1)
    qi = pl.program_id(2)
    tile_step = slope_ref[hd] * (LOG2E * TK)

    qs = (q1_ref[0, 0, 0], q2_ref[0, 0, 0])
    ks = (k1_ref, k2_ref)

    def scores(j, c):
        kt = ks[c][0, pl.ds(pl.multiple_of(j * TK, TK), TK), :]
        return jnp.dot(kt, qs[c], preferred_element_type=F32)

    def softmax(z, j, c, masked):
        if masked:
            krow = lax.broadcasted_iota(jnp.int32, (TK, TQ), 0)
            qcol = lax.broadcasted_iota(jnp.int32, (TK, TQ), 1)
            z = jnp.where(krow <= qcol, z, NEG_BIG)
        shift = tile_step * (j - qi).astype(F32)
        m_old = m_sc[c]
        m_new = jnp.maximum(m_old, jnp.max(z, axis=0, keepdims=True) + shift)
        m_sc[c] = m_new
        return jnp.exp2(z - (m_new - shift)).astype(BF16), jnp.exp2(m_old - m_new)

    def accumulate(c, j, p, alpha):
        acc_sc[c] = alpha * acc_sc[c] + jnp.dot(v_ref[0, 0, j], p, preferred_element_type=F32)

    m_sc[...] = jnp.full(m_sc.shape, NEG_BIG, F32)
    acc_sc[...] = jnp.zeros(acc_sc.shape, F32)
    a_sc[...] = jnp.ones(a_sc.shape, F32)
    p_sc[1] = jnp.zeros(p_sc.shape[1:], BF16)
    z_sc[0] = scores(0, 0)

    def step(j, par, last):
        zb = scores(j, 1)
        pa, alpha_a = softmax(z_sc[par], j, 0, masked=last)
        accumulate(1, jnp.maximum(j - 1, 0), p_sc[1 - par], a_sc[...])
        if not last:
            z_sc[1 - par] = scores(j + 1, 0)
        pb, alpha_b = softmax(zb, j, 1, masked=last)
        accumulate(0, j, pa, alpha_a)
        if last:
            accumulate(1, j, pb, alpha_b)
        else:
            p_sc[par] = pb
            a_sc[...] = alpha_b

    def pair(t, carry):
        step(2 * t, 0, last=False)
        step(2 * t + 1, 1, last=False)
        return carry

    lax.fori_loop(0, qi // 2, pair, 0)

    @pl.when(qi % 2 == 0)
    def _():
        step(qi, 0, last=True)

    @pl.when(qi % 2 == 1)
    def _():
        step(qi - 1, 0, last=False)
        step(qi, 1, last=True)

    lam = (jnp.exp(jnp.sum(lq1_ref[...] * lk1_ref[...], keepdims=True))
           - jnp.exp(jnp.sum(lq2_ref[...] * lk2_ref[...], keepdims=True))
           + lambda_init)
    acc_a, acc_b = acc_sc[0], acc_sc[1]
    ot = (acc_a[:HEAD_WIDTH] / acc_a[HEAD_WIDTH:HEAD_WIDTH + 1]
          - lam * (acc_b[:HEAD_WIDTH] / acc_b[HEAD_WIDTH:HEAD_WIDTH + 1]))
    ot = ot * lax.rsqrt(jnp.mean(ot * ot, axis=0, keepdims=True) + EPS) * subg_ref[...]
    o_ref[0] = (ot * (1.0 - lambda_init)).T.astype(BF16)


def _attention(slopes, q1, q2, k1, k2, v, lq1, lk1, lq2, lk2, subg, lambda_init):
    bsz, n_heads, n_tiles = q1.shape[:3]
    s, d = k1.shape[1], k1.shape[2]
    qspec = pl.BlockSpec((1, 1, 1, HEAD_WIDTH, TQ), lambda b, h, i: (b, h, i, 0, 0))
    kspec = pl.BlockSpec((1, s, HEAD_WIDTH), lambda b, h, i: (b, 0, h))
    vspec = pl.BlockSpec((1, 1, n_tiles, V_ROWS, TK), lambda b, h, i: (b, h, 0, 0, 0))
    small = lambda n: pl.BlockSpec((1, n), lambda b, h, i: (0, 0))
    return pl.pallas_call(
        functools.partial(_attn_kernel, lambda_init=lambda_init),
        grid=(bsz, n_heads, s // TQ),
        in_specs=[
            pl.BlockSpec(memory_space=pltpu.SMEM),
            qspec, qspec, kspec, kspec, vspec,
            small(HEAD_DIM), small(HEAD_DIM), small(HEAD_DIM), small(HEAD_DIM),
            pl.BlockSpec((HEAD_WIDTH, 1), lambda b, h, i: (0, 0)),
        ],
        out_specs=pl.BlockSpec((1, TQ, HEAD_WIDTH), lambda b, h, i: (b, i, h)),
        out_shape=jax.ShapeDtypeStruct((bsz, s, d), BF16),
        scratch_shapes=[
            pltpu.VMEM((2, TK, TQ), F32),
            pltpu.VMEM((2, TK, TQ), BF16),
            pltpu.VMEM((2, 1, TQ), F32),
            pltpu.VMEM((1, TQ), F32),
            pltpu.VMEM((2, V_ROWS, TQ), F32),
        ],
        compiler_params=pltpu.CompilerParams(
            dimension_semantics=("arbitrary", "arbitrary", "arbitrary"),
            vmem_limit_bytes=VMEM_LIMIT),
        name="diff_attn",
    )(slopes, q1, q2, k1, k2, v, lq1, lk1, lq2, lk2, subg)


def _proj_kernel(x_ref, a_ref, w_ref, o_ref):
    o_ref[...] = x_ref[...] + jnp.dot(a_ref[...], w_ref[...], preferred_element_type=F32)


def _out_proj(x2, a2, w):
    t, d = x2.shape
    row = pl.BlockSpec((TM, d), lambda i: (i, 0))
    return pl.pallas_call(
        _proj_kernel,
        grid=(t // TM,),
        in_specs=[row, row, _resident((d, d))],
        out_specs=row,
        out_shape=jax.ShapeDtypeStruct((t, d), F32),
        compiler_params=pltpu.CompilerParams(
            dimension_semantics=("arbitrary",), vmem_limit_bytes=VMEM_LIMIT),
        name="attn_out_proj",
    )(x2, a2, w)


def _row(v):
    return v.reshape(1, -1)


def _alibi_tables(n_heads):
    slopes = np.exp2(-8.0 * np.arange(1, n_heads + 1) / n_heads)
    assert np.all(np.log2(slopes) == np.round(np.log2(slopes))), "slopes must be exact in bf16"
    pos = np.arange(TM, dtype=np.float32) * np.float32(LOG2E)
    pieces = []
    for _ in range(N_POS_PIECES):
        piece = pos.astype(BF16).astype(np.float32)
        pieces.append(piece)
        pos = pos - piece
    cols = np.stack(pieces, axis=1)
    half = np.pad(cols, ((0, 0), (0, HEAD_DIM - N_POS_PIECES)))
    kext = np.tile(np.concatenate([half, half], axis=1), (1, n_heads))
    slot = (np.arange(HEAD_WIDTH) % HEAD_DIM) < N_POS_PIECES
    qext = (slopes[:, None] * slot[None, :]).reshape(-1, 1)
    return jnp.asarray(slopes, F32), jnp.asarray(kext, F32), jnp.asarray(qext, F32)


def kernel(x, ffn_norm, ffn_w_in, ffn_w_out, mix_norm, conv_w_in, conv_b_in, conv_dw, conv_dw_b, conv_ln_g, conv_ln_b, conv_w_out, conv_b_out, attn_w_qkv, attn_q_norm, attn_k_norm, attn_lq1, attn_lk1, attn_lq2, attn_lk2, attn_subln_g, attn_w_out):
    bsz, s, d = x.shape
    depth = ffn_norm.shape[0]
    f = ffn_w_out.shape[2]
    n_heads = d // HEAD_WIDTH
    n_mixers = 2

    def ffn(x2, i, j):
        w_in = ffn_w_in[i, j].astype(BF16)
        return _ffn(x2, _row(ffn_norm[i, j]), w_in[:, :f], w_in[:, f:], ffn_w_out[i, j].astype(BF16))

    x2 = x.reshape(bsz * s, d)
    for i in range(depth):
        x2 = ffn(x2, i, 0)
        j = i // n_mixers
        if i % n_mixers == 0:
            w_in = conv_w_in[j].astype(BF16)
            kpad = -(-CONV_WIDTH // 8) * 8
            dw3 = jnp.pad(conv_dw[j], ((0, kpad - CONV_WIDTH), (0, 0)))
            dw3 = dw3.reshape(kpad, d // LANES, LANES).transpose(1, 0, 2)
            x2 = _conv_mixer(
                x2.reshape(bsz, s, d), _row(mix_norm[i]), w_in[:, :d], w_in[:, d:],
                _row(conv_b_in[j][:d]), _row(conv_b_in[j][d:]), dw3, _row(conv_dw_b[j]),
                _row(conv_ln_g[j]), _row(conv_ln_b[j]), conv_w_out[j].astype(BF16),
                _row(conv_b_out[j])).reshape(bsz * s, d)
        else:
            lambda_init = 0.8 - 0.6 * math.exp(-0.3 * i)
            w_qkv = attn_w_qkv[j].astype(BF16)
            gq = _row(jnp.tile(attn_q_norm[j], d // HEAD_DIM)) * (LOG2E / math.sqrt(HEAD_DIM))
            gk = _row(jnp.tile(attn_k_norm[j], d // HEAD_DIM))
            group = jnp.arange(d) // HEAD_DIM
            gmat = (group[:, None] == group[None, :]).astype(BF16)
            slopes, kext, qext = _alibi_tables(n_heads)
            q1, q2, k1, k2, v = _qkv(x2, _row(mix_norm[i]), w_qkv[:, :d], w_qkv[:, d:2 * d], w_qkv[:, 2 * d:],
                                     gq, gk, gmat, kext, qext, bsz)
            o = _attention(slopes, q1, q2, k1.reshape(bsz, s, d), k2.reshape(bsz, s, d), v,
                           _row(attn_lq1[j]), _row(attn_lk1[j]), _row(attn_lq2[j]), _row(attn_lk2[j]),
                           attn_subln_g[j].reshape(-1, 1), lambda_init)
            x2 = _out_proj(x2, o.reshape(bsz * s, d), attn_w_out[j].astype(BF16))
        x2 = ffn(x2, i, 1)
    return x2.reshape(bsz, s, d)
```

```python
import functools
import math

import jax
import jax.numpy as jnp
import numpy as np
from jax import lax
from jax.experimental import pallas as pl
from jax.experimental.pallas import tpu as pltpu

EPS = 1e-6
CONV_WIDTH = 31
HEAD_DIM = 64
HEAD_WIDTH = 2 * HEAD_DIM
LANES = 128
MXU_COLS = 256
BF16_SUBLANES = 16
V_ROWS = HEAD_WIDTH + BF16_SUBLANES
N_POS_PIECES = 3
HALO = 32
NEG_BIG = -1e30
LOG2E = 1.4426950408889634

TM = 512
TQ = 512
TK = 512
assert TM == TK == TQ
CONV_ROWS = 64
VMEM_LIMIT = 56 * 1024 * 1024

F32 = jnp.float32
BF16 = jnp.bfloat16


def _rms(x, g):
    return x * lax.rsqrt(jnp.mean(x * x, axis=-1, keepdims=True) + EPS) * g


def _resident(shape, lead=()):
    index = tuple(lead) + (0,) * len(shape)
    return pl.BlockSpec((None,) * len(lead) + tuple(shape), lambda *_: index, pipeline_mode=pl.Buffered(1))


def _ffn_chunks(f, n_chunks):
    tiles = f // MXU_COLS
    assert tiles * MXU_COLS == f
    edges = [MXU_COLS * (tiles * c // n_chunks) for c in range(n_chunks + 1)]
    return list(zip(edges[:-1], edges[1:]))


def _ffn_kernel(*refs, n_chunks, project):
    if project:
        x_ref, a_ref, wp_ref, g_ref, win_ref, wo_ref, o_ref = refs
        x = x_ref[...] + jnp.dot(a_ref[...], wp_ref[...], preferred_element_type=F32)
    else:
        x_ref, g_ref, win_ref, wo_ref, o_ref = refs
        x = x_ref[...]
    f = wo_ref.shape[0]
    h = _rms(x, g_ref[...]).astype(BF16)
    y = jnp.zeros(x.shape, F32)
    for lo, hi in _ffn_chunks(f, n_chunks):
        gate = jnp.dot(h, win_ref[:, lo:hi], preferred_element_type=F32)
        up = jnp.dot(h, win_ref[:, f + lo:f + hi], preferred_element_type=F32)
        act = (gate * jax.nn.sigmoid(gate) * up).astype(BF16)
        y = y + jnp.dot(act, wo_ref[lo:hi, :], preferred_element_type=F32)
    o_ref[...] = x + 0.5 * y


def _ffn(x2, g, w_in, w_out, which, proj=None):
    t, d = x2.shape
    f = w_out.shape[-2]
    row = pl.BlockSpec((TM, d), lambda i: (i, 0))
    proj_specs = [] if proj is None else [row, _resident((d, d))]
    proj_args = () if proj is None else proj
    return pl.pallas_call(
        functools.partial(_ffn_kernel, n_chunks=2, project=proj is not None),
        grid=(t // TM,),
        in_specs=[row, *proj_specs, _resident((1, d)),
                  _resident((d, 2 * f), which), _resident((f, d), which)],
        out_specs=row,
        out_shape=jax.ShapeDtypeStruct((t, d), F32),
        compiler_params=pltpu.CompilerParams(
            dimension_semantics=("arbitrary",), vmem_limit_bytes=VMEM_LIMIT),
        name="ffn" if proj is None else "proj_ffn",
    )(x2, *proj_args, g, w_in, w_out)


def _conv_kernel(x_ref, g_ref, win_ref, bin_ref, dw_ref, dwb_ref,
                 lng_ref, lnb_ref, wo_ref, bo_ref, o_ref, ubuf, cbuf):
    tm, d = x_ref.shape[1], x_ref.shape[2]
    n_lane_chunks = d // LANES

    @pl.when(pl.program_id(1) == 0)
    def _():
        ubuf[:, 0:HALO, :] = jnp.zeros((n_lane_chunks, HALO, LANES), F32)

    x = x_ref[0]
    h = _rms(x, g_ref[...]).astype(BF16)
    a = jnp.dot(h, win_ref[:, 0:d], preferred_element_type=F32) + bin_ref[:, 0:d]
    b = jnp.dot(h, win_ref[:, d:2 * d], preferred_element_type=F32) + bin_ref[:, d:2 * d]
    u = a * jax.nn.sigmoid(b)
    for c in range(n_lane_chunks):
        ubuf[c, HALO:HALO + tm, :] = u[:, c * LANES:(c + 1) * LANES]

    def lane_chunk(c, carry):
        for r0 in range(0, tm, CONV_ROWS):
            acc = jnp.zeros((CONV_ROWS, LANES), F32)
            for k in range(CONV_WIDTH):
                start = HALO + r0 - (CONV_WIDTH - 1) + k
                acc = acc + dw_ref[c, k:k + 1, :] * ubuf[c, start:start + CONV_ROWS, :]
            cbuf[c, r0:r0 + CONV_ROWS, :] = acc
        ubuf[c, 0:HALO, :] = ubuf[c, tm:tm + HALO, :]
        return carry

    lax.fori_loop(0, n_lane_chunks, lane_chunk, 0)

    y = jnp.concatenate([cbuf[c] for c in range(n_lane_chunks)], axis=1) + dwb_ref[...]
    mu = jnp.mean(y, axis=-1, keepdims=True)
    yc = y - mu
    var = jnp.mean(yc * yc, axis=-1, keepdims=True)
    z = yc * lax.rsqrt(var + EPS) * lng_ref[...] + lnb_ref[...]
    z = (z * jax.nn.sigmoid(z)).astype(BF16)
    m = jnp.dot(z, wo_ref[...], preferred_element_type=F32) + bo_ref[...]
    o_ref[0] = x + m


def _conv_mixer(x3, g, w_in, b_in, dw3, dwb, lng, lnb, wo, bo):
    bsz, s, d = x3.shape
    nl = d // LANES
    kpad = dw3.shape[1]
    return pl.pallas_call(
        _conv_kernel,
        grid=(bsz, s // TM),
        in_specs=[
            pl.BlockSpec((1, TM, d), lambda b, i: (b, i, 0)),
            _resident((1, d)),
            _resident((d, 2 * d)), _resident((1, 2 * d)),
            _resident((nl, kpad, LANES)),
            _resident((1, d)), _resident((1, d)), _resident((1, d)),
            _resident((d, d)), _resident((1, d)),
        ],
        out_specs=pl.BlockSpec((1, TM, d), lambda b, i: (b, i, 0)),
        out_shape=jax.ShapeDtypeStruct((bsz, s, d), F32),
        scratch_shapes=[
            pltpu.VMEM((nl, HALO + TM, LANES), F32),
            pltpu.VMEM((nl, TM, LANES), F32),
        ],
        compiler_params=pltpu.CompilerParams(
            dimension_semantics=("arbitrary", "arbitrary"), vmem_limit_bytes=VMEM_LIMIT),
        name="conv_mixer",
    )(x3, g, w_in, b_in, dw3, dwb, lng, lnb, wo, bo)


def _split_dot(v, w_ref):
    hi = v.astype(BF16)
    lo = (v - hi.astype(F32)).astype(BF16)
    return (jnp.dot(hi, w_ref[...], preferred_element_type=F32)
            + jnp.dot(lo, w_ref[...], preferred_element_type=F32))


def _group_rsqrt(v, gsum_ref, gexp_ref):
    ms = _split_dot(v * v, gsum_ref) * (1.0 / HEAD_DIM)
    return _split_dot(lax.rsqrt(ms + EPS), gexp_ref)


def _qkv_kernel(x_ref, g_ref, w_ref, gq_ref, gk_ref, gsum_ref, gexp_ref, kext_ref, qext_ref,
                q1_ref, q2_ref, k1_ref, k2_ref, v_ref):
    n_heads, tm = q1_ref.shape[1], q1_ref.shape[4]
    d = x_ref.shape[1]
    x = x_ref[...]
    h = _rms(x, g_ref[...]).astype(BF16)
    q = jnp.dot(h, w_ref[:, 0:d], preferred_element_type=F32)
    k = jnp.dot(h, w_ref[:, d:2 * d], preferred_element_type=F32)
    v = jnp.dot(h, w_ref[:, 2 * d:3 * d], preferred_element_type=F32)
    qn = q * _group_rsqrt(q, gsum_ref, gexp_ref) * gq_ref[...]
    kn = k * _group_rsqrt(k, gsum_ref, gexp_ref) * gk_ref[...]
    lane_first = (lax.broadcasted_iota(jnp.int32, kn.shape, 1) % HEAD_WIDTH) < HEAD_DIM
    kext = kext_ref[...]
    k1_ref[...] = jnp.where(lane_first, kn, kext).astype(BF16)
    k2_ref[...] = jnp.where(lane_first, kext, kn).astype(BF16)
    qt = qn.T.reshape(n_heads, HEAD_WIDTH, tm)
    row_first = lax.broadcasted_iota(jnp.int32, qt.shape, 1) < HEAD_DIM
    qext = jnp.broadcast_to(qext_ref[...].reshape(n_heads, HEAD_WIDTH, 1), qt.shape)
    q1_ref[0, :, 0] = jnp.where(row_first, qt, qext).astype(BF16)
    q2_ref[0, :, 0] = jnp.where(row_first, qext, qt).astype(BF16)
    v_ref[0, :, 0, 0:HEAD_WIDTH, :] = v.T.reshape(n_heads, HEAD_WIDTH, tm).astype(BF16)
    ones_row = lax.broadcasted_iota(jnp.int32, (n_heads, V_ROWS - HEAD_WIDTH, tm), 1) == 0
    v_ref[0, :, 0, HEAD_WIDTH:V_ROWS, :] = jnp.where(ones_row, 1.0, 0.0).astype(BF16)


def _qkv(x2, g, w_qkv, gq, gk, gsum, gexp, kext, qext, bsz):
    t, d = x2.shape
    n_heads = d // HEAD_WIDTH
    n_tiles = t // bsz // TM
    row = pl.BlockSpec((TM, d), lambda i: (i, 0))
    slab = lambda rows: pl.BlockSpec((1, n_heads, 1, rows, TM), lambda i: (i // n_tiles, 0, i % n_tiles, 0, 0))
    slab_shape = lambda rows: jax.ShapeDtypeStruct((bsz, n_heads, n_tiles, rows, TM), BF16)
    rows_shape = jax.ShapeDtypeStruct((t, d), BF16)
    return pl.pallas_call(
        _qkv_kernel,
        grid=(t // TM,),
        in_specs=[row, _resident((1, d)), _resident((d, 3 * d)),
                  _resident((1, d)), _resident((1, d)), _resident((d, LANES)), _resident((LANES, d)),
                  _resident((TM, d)), _resident((d, 1))],
        out_specs=[slab(HEAD_WIDTH), slab(HEAD_WIDTH), row, row, slab(V_ROWS)],
        out_shape=[slab_shape(HEAD_WIDTH), slab_shape(HEAD_WIDTH), rows_shape, rows_shape, slab_shape(V_ROWS)],
        compiler_params=pltpu.CompilerParams(
            dimension_semantics=("arbitrary",), vmem_limit_bytes=VMEM_LIMIT),
        name="qkv_proj",
    )(x2, g, w_qkv, gq, gk, gsum, gexp, kext, qext)


def _attn_kernel(slope_ref, q1_ref, q2_ref, k1_ref, k2_ref, v_ref, lq1_ref, lk1_ref, lq2_ref, lk2_ref,
                 subg_ref, o_ref, z_sc, p_sc, m_sc, a_sc, acc_sc, *, lambda_init):
    hd = pl.program_id(1)
    qi = pl.program_id(2)
    tile_step = slope_ref[hd] * (LOG2E * TK)

    qs = (q1_ref[0, 0, 0], q2_ref[0, 0, 0])
    ks = (k1_ref, k2_ref)

    def scores(j, c):
        kt = ks[c][0, pl.ds(pl.multiple_of(j * TK, TK), TK), :]
        return jnp.dot(kt, qs[c], preferred_element_type=F32)

    def softmax(z, j, c, masked):
        if masked:
            krow = lax.broadcasted_iota(jnp.int32, (TK, TQ), 0)
            qcol = lax.broadcasted_iota(jnp.int32, (TK, TQ), 1)
            z = jnp.where(krow <= qcol, z, NEG_BIG)
        shift = tile_step * (j - qi).astype(F32)
        m_old = m_sc[c]
        m_new = jnp.maximum(m_old, jnp.max(z, axis=0, keepdims=True) + shift)
        m_sc[c] = m_new
        return jnp.exp2(z - (m_new - shift)).astype(BF16), jnp.exp2(m_old - m_new)

    def accumulate(c, j, p, alpha):
        acc_sc[c] = alpha * acc_sc[c] + jnp.dot(v_ref[0, 0, j], p, preferred_element_type=F32)

    m_sc[...] = jnp.full(m_sc.shape, NEG_BIG, F32)
    acc_sc[...] = jnp.zeros(acc_sc.shape, F32)
    a_sc[...] = jnp.ones(a_sc.shape, F32)
    p_sc[1] = jnp.zeros(p_sc.shape[1:], BF16)
    z_sc[0] = scores(0, 0)

    def step(j, par, last):
        zb = scores(j, 1)
        pa, alpha_a = softmax(z_sc[par], j, 0, masked=last)
        accumulate(1, jnp.maximum(j - 1, 0), p_sc[1 - par], a_sc[...])
        if not last:
            z_sc[1 - par] = scores(j + 1, 0)
        pb, alpha_b = softmax(zb, j, 1, masked=last)
        accumulate(0, j, pa, alpha_a)
        if last:
            accumulate(1, j, pb, alpha_b)
        else:
            p_sc[par] = pb
            a_sc[...] = alpha_b

    def pair(t, carry):
        step(2 * t, 0, last=False)
        step(2 * t + 1, 1, last=False)
        return carry

    lax.fori_loop(0, qi // 2, pair, 0)

    @pl.when(qi % 2 == 0)
    def _():
        step(qi, 0, last=True)

    @pl.when(qi % 2 == 1)
    def _():
        step(qi - 1, 0, last=False)
        step(qi, 1, last=True)

    lam = (jnp.exp(jnp.sum(lq1_ref[...] * lk1_ref[...], keepdims=True))
           - jnp.exp(jnp.sum(lq2_ref[...] * lk2_ref[...], keepdims=True))
           + lambda_init)
    acc_a, acc_b = acc_sc[0], acc_sc[1]
    ot = (acc_a[:HEAD_WIDTH] / acc_a[HEAD_WIDTH:HEAD_WIDTH + 1]
          - lam * (acc_b[:HEAD_WIDTH] / acc_b[HEAD_WIDTH:HEAD_WIDTH + 1]))
    ot = ot * lax.rsqrt(jnp.mean(ot * ot, axis=0, keepdims=True) + EPS) * subg_ref[...]
    o_ref[0] = (ot * (1.0 - lambda_init)).T.astype(BF16)


def _attention(slopes, q1, q2, k1, k2, v, lq1, lk1, lq2, lk2, subg, lambda_init):
    bsz, n_heads, n_tiles = q1.shape[:3]
    s, d = k1.shape[1], k1.shape[2]
    qspec = pl.BlockSpec((1, 1, 1, HEAD_WIDTH, TQ), lambda b, h, i: (b, h, i, 0, 0))
    kspec = pl.BlockSpec((1, s, HEAD_WIDTH), lambda b, h, i: (b, 0, h))
    vspec = pl.BlockSpec((1, 1, n_tiles, V_ROWS, TK), lambda b, h, i: (b, h, 0, 0, 0))
    small = lambda n: pl.BlockSpec((1, n), lambda b, h, i: (0, 0))
    return pl.pallas_call(
        functools.partial(_attn_kernel, lambda_init=lambda_init),
        grid=(bsz, n_heads, s // TQ),
        in_specs=[
            pl.BlockSpec(memory_space=pltpu.SMEM),
            qspec, qspec, kspec, kspec, vspec,
            small(HEAD_DIM), small(HEAD_DIM), small(HEAD_DIM), small(HEAD_DIM),
            pl.BlockSpec((HEAD_WIDTH, 1), lambda b, h, i: (0, 0)),
        ],
        out_specs=pl.BlockSpec((1, TQ, HEAD_WIDTH), lambda b, h, i: (b, i, h)),
        out_shape=jax.ShapeDtypeStruct((bsz, s, d), BF16),
        scratch_shapes=[
            pltpu.VMEM((2, TK, TQ), F32),
            pltpu.VMEM((2, TK, TQ), BF16),
            pltpu.VMEM((2, 1, TQ), F32),
            pltpu.VMEM((1, TQ), F32),
            pltpu.VMEM((2, V_ROWS, TQ), F32),
        ],
        compiler_params=pltpu.CompilerParams(
            dimension_semantics=("arbitrary", "arbitrary", "arbitrary"),
            vmem_limit_bytes=VMEM_LIMIT),
        name="diff_attn",
    )(slopes, q1, q2, k1, k2, v, lq1, lk1, lq2, lk2, subg)


def _row(v):
    return v.reshape(1, -1)


def _alibi_tables(n_heads):
    slopes = np.exp2(-8.0 * np.arange(1, n_heads + 1) / n_heads)
    assert np.all(np.log2(slopes) == np.round(np.log2(slopes))), "slopes must be exact in bf16"
    pos = np.arange(TM, dtype=np.float32) * np.float32(LOG2E)
    pieces = []
    for _ in range(N_POS_PIECES):
        piece = pos.astype(BF16).astype(np.float32)
        pieces.append(piece)
        pos = pos - piece
    cols = np.stack(pieces, axis=1)
    half = np.pad(cols, ((0, 0), (0, HEAD_DIM - N_POS_PIECES)))
    kext = np.tile(np.concatenate([half, half], axis=1), (1, n_heads))
    slot = (np.arange(HEAD_WIDTH) % HEAD_DIM) < N_POS_PIECES
    qext = (slopes[:, None] * slot[None, :]).reshape(-1, 1)
    return jnp.asarray(slopes, F32), jnp.asarray(kext, F32), jnp.asarray(qext, F32)


def _group_tables(d):
    group = np.arange(d) // HEAD_DIM
    gsum = (group[:, None] == np.arange(LANES)[None, :]).astype(np.float32)
    return jnp.asarray(gsum, BF16), jnp.asarray(gsum.T, BF16)


def kernel(x, ffn_norm, ffn_w_in, ffn_w_out, mix_norm, conv_w_in, conv_b_in, conv_dw, conv_dw_b, conv_ln_g, conv_ln_b, conv_w_out, conv_b_out, attn_w_qkv, attn_q_norm, attn_k_norm, attn_lq1, attn_lk1, attn_lq2, attn_lk2, attn_subln_g, attn_w_out):
    bsz, s, d = x.shape
    depth = ffn_norm.shape[0]
    n_heads = d // HEAD_WIDTH
    n_mixers = 2
    ffn_w_in = ffn_w_in.astype(BF16)
    ffn_w_out = ffn_w_out.astype(BF16)

    def ffn(x2, i, j, proj=None):
        return _ffn(x2, _row(ffn_norm[i, j]), ffn_w_in, ffn_w_out, (i, j), proj)

    x2 = x.reshape(bsz * s, d)
    for i in range(depth):
        x2 = ffn(x2, i, 0)
        j = i // n_mixers
        if i % n_mixers == 0:
            kpad = -(-CONV_WIDTH // 8) * 8
            dw3 = jnp.pad(conv_dw[j], ((0, kpad - CONV_WIDTH), (0, 0)))
            dw3 = dw3.reshape(kpad, d // LANES, LANES).transpose(1, 0, 2)
            x2 = _conv_mixer(
                x2.reshape(bsz, s, d), _row(mix_norm[i]), conv_w_in[j].astype(BF16), _row(conv_b_in[j]),
                dw3, _row(conv_dw_b[j]), _row(conv_ln_g[j]), _row(conv_ln_b[j]),
                conv_w_out[j].astype(BF16), _row(conv_b_out[j])).reshape(bsz * s, d)
            x2 = ffn(x2, i, 1)
        else:
            lambda_init = 0.8 - 0.6 * math.exp(-0.3 * i)
            gq = _row(jnp.tile(attn_q_norm[j], d // HEAD_DIM)) * (LOG2E / math.sqrt(HEAD_DIM))
            gk = _row(jnp.tile(attn_k_norm[j], d // HEAD_DIM))
            gsum, gexp = _group_tables(d)
            slopes, kext, qext = _alibi_tables(n_heads)
            q1, q2, k1, k2, v = _qkv(x2, _row(mix_norm[i]), attn_w_qkv[j].astype(BF16),
                                     gq, gk, gsum, gexp, kext, qext, bsz)
            o = _attention(slopes, q1, q2, k1.reshape(bsz, s, d), k2.reshape(bsz, s, d), v,
                           _row(attn_lq1[j]), _row(attn_lk1[j]), _row(attn_lq2[j]), _row(attn_lk2[j]),
                           attn_subln_g[j].reshape(-1, 1), lambda_init)
            x2 = ffn(x2, i, 1, proj=(o.reshape(bsz * s, d), attn_w_out[j].astype(BF16)))
    return x2.reshape(bsz, s, d)
```

```python
import functools
import math

import jax
import jax.numpy as jnp
import numpy as np
from jax import lax
from jax.experimental import pallas as pl
from jax.experimental.pallas import tpu as pltpu

EPS = 1e-6
CONV_WIDTH = 31
HEAD_DIM = 64
HEAD_WIDTH = 2 * HEAD_DIM
LANES = 128
MXU_COLS = 256
BF16_SUBLANES = 16
V_ROWS = HEAD_WIDTH + BF16_SUBLANES
N_POS_PIECES = 3
HALO = 32
NEG_BIG = -1e30
LOG2E = 1.4426950408889634
BOUND_MARGIN = 1.02
MAX_SAFE_BOUND = 40.0

TM = 512
TQ = 512
TK = 512
assert TM == TK == TQ
BOUNDED_GROUP = 4
CONV_ROWS = 64
VMEM_LIMIT = 56 * 1024 * 1024

F32 = jnp.float32
BF16 = jnp.bfloat16


def _rms(x, g):
    return x * lax.rsqrt(jnp.mean(x * x, axis=-1, keepdims=True) + EPS) * g


def _resident(shape, lead=()):
    index = tuple(lead) + (0,) * len(shape)
    return pl.BlockSpec((None,) * len(lead) + tuple(shape), lambda *_: index, pipeline_mode=pl.Buffered(1))


def _ffn_chunks(f, n_chunks):
    tiles = f // MXU_COLS
    assert tiles * MXU_COLS == f
    edges = [MXU_COLS * (tiles * c // n_chunks) for c in range(n_chunks + 1)]
    return list(zip(edges[:-1], edges[1:]))


def _ffn_kernel(*refs, n_chunks, project):
    if project:
        x_ref, a_ref, wp_ref, g_ref, win_ref, wo_ref, o_ref = refs
        x = x_ref[...] + jnp.dot(a_ref[...], wp_ref[...], preferred_element_type=F32)
    else:
        x_ref, g_ref, win_ref, wo_ref, o_ref = refs
        x = x_ref[...]
    f = wo_ref.shape[0]
    h = _rms(x, g_ref[...]).astype(BF16)
    y = jnp.zeros(x.shape, F32)
    for lo, hi in _ffn_chunks(f, n_chunks):
        gate = jnp.dot(h, win_ref[:, lo:hi], preferred_element_type=F32)
        up = jnp.dot(h, win_ref[:, f + lo:f + hi], preferred_element_type=F32)
        act = (gate * jax.nn.sigmoid(gate) * up).astype(BF16)
        y = y + jnp.dot(act, wo_ref[lo:hi, :], preferred_element_type=F32)
    o_ref[...] = x + 0.5 * y


def _ffn(x2, g, w_in, w_out, which, proj=None):
    t, d = x2.shape
    f = w_out.shape[-2]
    row = pl.BlockSpec((TM, d), lambda i: (i, 0))
    proj_specs = [] if proj is None else [row, _resident((d, d))]
    proj_args = () if proj is None else proj
    return pl.pallas_call(
        functools.partial(_ffn_kernel, n_chunks=2, project=proj is not None),
        grid=(t // TM,),
        in_specs=[row, *proj_specs, _resident((1, d)),
                  _resident((d, 2 * f), which), _resident((f, d), which)],
        out_specs=row,
        out_shape=jax.ShapeDtypeStruct((t, d), F32),
        compiler_params=pltpu.CompilerParams(
            dimension_semantics=("arbitrary",), vmem_limit_bytes=VMEM_LIMIT),
        name="ffn" if proj is None else "proj_ffn",
    )(x2, *proj_args, g, w_in, w_out)


def _conv_kernel(x_ref, g_ref, win_ref, bin_ref, dw_ref, dwb_ref,
                 lng_ref, lnb_ref, wo_ref, bo_ref, o_ref, ubuf, cbuf):
    tm, d = x_ref.shape[1], x_ref.shape[2]
    n_lane_chunks = d // LANES

    @pl.when(pl.program_id(1) == 0)
    def _():
        ubuf[:, 0:HALO, :] = jnp.zeros((n_lane_chunks, HALO, LANES), F32)

    x = x_ref[0]
    h = _rms(x, g_ref[...]).astype(BF16)
    a = jnp.dot(h, win_ref[:, 0:d], preferred_element_type=F32) + bin_ref[:, 0:d]
    b = jnp.dot(h, win_ref[:, d:2 * d], preferred_element_type=F32) + bin_ref[:, d:2 * d]
    u = a * jax.nn.sigmoid(b)
    for c in range(n_lane_chunks):
        ubuf[c, HALO:HALO + tm, :] = u[:, c * LANES:(c + 1) * LANES]

    def lane_chunk(c, carry):
        for r0 in range(0, tm, CONV_ROWS):
            acc = jnp.zeros((CONV_ROWS, LANES), F32)
            for k in range(CONV_WIDTH):
                start = HALO + r0 - (CONV_WIDTH - 1) + k
                acc = acc + dw_ref[c, k:k + 1, :] * ubuf[c, start:start + CONV_ROWS, :]
            cbuf[c, r0:r0 + CONV_ROWS, :] = acc
        ubuf[c, 0:HALO, :] = ubuf[c, tm:tm + HALO, :]
        return carry

    lax.fori_loop(0, n_lane_chunks, lane_chunk, 0)

    y = jnp.concatenate([cbuf[c] for c in range(n_lane_chunks)], axis=1) + dwb_ref[...]
    mu = jnp.mean(y, axis=-1, keepdims=True)
    yc = y - mu
    var = jnp.mean(yc * yc, axis=-1, keepdims=True)
    z = yc * lax.rsqrt(var + EPS) * lng_ref[...] + lnb_ref[...]
    z = (z * jax.nn.sigmoid(z)).astype(BF16)
    m = jnp.dot(z, wo_ref[...], preferred_element_type=F32) + bo_ref[...]
    o_ref[0] = x + m


def _conv_mixer(x3, g, w_in, b_in, dw3, dwb, lng, lnb, wo, bo):
    bsz, s, d = x3.shape
    nl = d // LANES
    kpad = dw3.shape[1]
    return pl.pallas_call(
        _conv_kernel,
        grid=(bsz, s // TM),
        in_specs=[
            pl.BlockSpec((1, TM, d), lambda b, i: (b, i, 0)),
            _resident((1, d)),
            _resident((d, 2 * d)), _resident((1, 2 * d)),
            _resident((nl, kpad, LANES)),
            _resident((1, d)), _resident((1, d)), _resident((1, d)),
            _resident((d, d)), _resident((1, d)),
        ],
        out_specs=pl.BlockSpec((1, TM, d), lambda b, i: (b, i, 0)),
        out_shape=jax.ShapeDtypeStruct((bsz, s, d), F32),
        scratch_shapes=[
            pltpu.VMEM((nl, HALO + TM, LANES), F32),
            pltpu.VMEM((nl, TM, LANES), F32),
        ],
        compiler_params=pltpu.CompilerParams(
            dimension_semantics=("arbitrary", "arbitrary"), vmem_limit_bytes=VMEM_LIMIT),
        name="conv_mixer",
    )(x3, g, w_in, b_in, dw3, dwb, lng, lnb, wo, bo)


def _split_dot(v, w_ref):
    hi = v.astype(BF16)
    lo = (v - hi.astype(F32)).astype(BF16)
    return (jnp.dot(hi, w_ref[...], preferred_element_type=F32)
            + jnp.dot(lo, w_ref[...], preferred_element_type=F32))


def _group_rsqrt(v, gsum_ref, gexp_ref):
    ms = _split_dot(v * v, gsum_ref) * (1.0 / HEAD_DIM)
    return _split_dot(lax.rsqrt(ms + EPS), gexp_ref)


def _qkv_kernel(x_ref, g_ref, w_ref, gq_ref, gk_ref, gsum_ref, gexp_ref, kext_ref, qext_ref,
                q1_ref, q2_ref, k1_ref, k2_ref, v_ref):
    n_heads, tm = q1_ref.shape[1], q1_ref.shape[4]
    d = x_ref.shape[1]
    x = x_ref[...]
    h = _rms(x, g_ref[...]).astype(BF16)
    q = jnp.dot(h, w_ref[:, 0:d], preferred_element_type=F32)
    k = jnp.dot(h, w_ref[:, d:2 * d], preferred_element_type=F32)
    v = jnp.dot(h, w_ref[:, 2 * d:3 * d], preferred_element_type=F32)
    qn = q * _group_rsqrt(q, gsum_ref, gexp_ref) * gq_ref[...]
    kn = k * _group_rsqrt(k, gsum_ref, gexp_ref) * gk_ref[...]
    lane_first = (lax.broadcasted_iota(jnp.int32, kn.shape, 1) % HEAD_WIDTH) < HEAD_DIM
    kext = kext_ref[...]
    k1_ref[...] = jnp.where(lane_first, kn, kext).astype(BF16)
    k2_ref[...] = jnp.where(lane_first, kext, kn).astype(BF16)
    qt = qn.T.reshape(n_heads, HEAD_WIDTH, tm)
    row_first = lax.broadcasted_iota(jnp.int32, qt.shape, 1) < HEAD_DIM
    qext = jnp.broadcast_to(qext_ref[...].reshape(n_heads, HEAD_WIDTH, 1), qt.shape)
    q1_ref[0, :, 0] = jnp.where(row_first, qt, qext).astype(BF16)
    q2_ref[0, :, 0] = jnp.where(row_first, qext, qt).astype(BF16)
    v_ref[0, :, 0, 0:HEAD_WIDTH, :] = v.T.reshape(n_heads, HEAD_WIDTH, tm).astype(BF16)
    ones_row = lax.broadcasted_iota(jnp.int32, (n_heads, V_ROWS - HEAD_WIDTH, tm), 1) == 0
    v_ref[0, :, 0, HEAD_WIDTH:V_ROWS, :] = jnp.where(ones_row, 1.0, 0.0).astype(BF16)


def _qkv(x2, g, w_qkv, gq, gk, gsum, gexp, kext, qext, bsz):
    t, d = x2.shape
    n_heads = d // HEAD_WIDTH
    n_tiles = t // bsz // TM
    row = pl.BlockSpec((TM, d), lambda i: (i, 0))
    slab = lambda rows: pl.BlockSpec((1, n_heads, 1, rows, TM), lambda i: (i // n_tiles, 0, i % n_tiles, 0, 0))
    slab_shape = lambda rows: jax.ShapeDtypeStruct((bsz, n_heads, n_tiles, rows, TM), BF16)
    rows_shape = jax.ShapeDtypeStruct((t, d), BF16)
    return pl.pallas_call(
        _qkv_kernel,
        grid=(t // TM,),
        in_specs=[row, _resident((1, d)), _resident((d, 3 * d)),
                  _resident((1, d)), _resident((1, d)), _resident((d, LANES)), _resident((LANES, d)),
                  _resident((TM, d)), _resident((d, 1))],
        out_specs=[slab(HEAD_WIDTH), slab(HEAD_WIDTH), row, row, slab(V_ROWS)],
        out_shape=[slab_shape(HEAD_WIDTH), slab_shape(HEAD_WIDTH), rows_shape, rows_shape, slab_shape(V_ROWS)],
        compiler_params=pltpu.CompilerParams(
            dimension_semantics=("arbitrary",), vmem_limit_bytes=VMEM_LIMIT),
        name="qkv_proj",
    )(x2, g, w_qkv, gq, gk, gsum, gexp, kext, qext)


def _attn_kernel(slope_ref, bound_ref, q1_ref, q2_ref, k1_ref, k2_ref, v_ref, lq1_ref, lk1_ref, lq2_ref,
                 lk2_ref, subg_ref, o_ref, z_sc, p_sc, m_sc, a_sc, acc_sc, *, lambda_init):
    hd = pl.program_id(1)
    qi = pl.program_id(2)
    slope = slope_ref[hd] * LOG2E
    tile_step = slope * TK

    qs = (q1_ref[0, 0, 0], q2_ref[0, 0, 0])
    ks = (k1_ref, k2_ref)

    def scores(j, c):
        kt = ks[c][0, pl.ds(pl.multiple_of(j * TK, TK), TK), :]
        return jnp.dot(kt, qs[c], preferred_element_type=F32)

    def causal(z):
        krow = lax.broadcasted_iota(jnp.int32, (TK, TQ), 0)
        qcol = lax.broadcasted_iota(jnp.int32, (TK, TQ), 1)
        return jnp.where(krow <= qcol, z, NEG_BIG)

    acc_sc[...] = jnp.zeros(acc_sc.shape, F32)

    def bounded_tiles(j0, n_tiles, diagonal_last):
        qoff = lax.broadcasted_iota(jnp.int32, (1, TQ), 1).astype(F32)
        ref = bound_ref[0] + slope * qoff
        for c in range(2):
            ps = []
            for t in range(n_tiles):
                z = scores(j0 + t, c)
                if diagonal_last and t == n_tiles - 1:
                    z = causal(z)
                off = tile_step * (j0 + t - qi).astype(F32) - ref
                ps.append(jnp.exp2(z + off).astype(BF16))
            vt = jnp.concatenate([v_ref[0, 0, j0 + t] for t in range(n_tiles)], axis=1)
            acc_sc[c] += jnp.dot(vt, jnp.concatenate(ps, axis=0), preferred_element_type=F32)

    def bounded_path():
        def group(t, carry):
            bounded_tiles(BOUNDED_GROUP * t, BOUNDED_GROUP, diagonal_last=False)
            return carry

        lax.fori_loop(0, qi // BOUNDED_GROUP, group, 0)
        for rest in range(BOUNDED_GROUP):
            pl.when(qi % BOUNDED_GROUP == rest)(
                functools.partial(bounded_tiles, qi - rest, rest + 1, diagonal_last=True))

    def softmax(z, j, c, masked):
        if masked:
            z = causal(z)
        shift = tile_step * (j - qi).astype(F32)
        m_old = m_sc[c]
        m_new = jnp.maximum(m_old, jnp.max(z, axis=0, keepdims=True) + shift)
        m_sc[c] = m_new
        return jnp.exp2(z - (m_new - shift)).astype(BF16), jnp.exp2(m_old - m_new)

    def accumulate(c, j, p, alpha):
        acc_sc[c] = alpha * acc_sc[c] + jnp.dot(v_ref[0, 0, j], p, preferred_element_type=F32)

    def step(j, par, last):
        zb = scores(j, 1)
        pa, alpha_a = softmax(z_sc[par], j, 0, masked=last)
        accumulate(1, jnp.maximum(j - 1, 0), p_sc[1 - par], a_sc[...])
        if not last:
            z_sc[1 - par] = scores(j + 1, 0)
        pb, alpha_b = softmax(zb, j, 1, masked=last)
        accumulate(0, j, pa, alpha_a)
        if last:
            accumulate(1, j, pb, alpha_b)
        else:
            p_sc[par] = pb
            a_sc[...] = alpha_b

    def running_max_path():
        m_sc[...] = jnp.full(m_sc.shape, NEG_BIG, F32)
        a_sc[...] = jnp.ones(a_sc.shape, F32)
        p_sc[1] = jnp.zeros(p_sc.shape[1:], BF16)
        z_sc[0] = scores(0, 0)

        def pair(t, carry):
            step(2 * t, 0, last=False)
            step(2 * t + 1, 1, last=False)
            return carry

        lax.fori_loop(0, qi // 2, pair, 0)

        @pl.when(qi % 2 == 0)
        def _():
            step(qi, 0, last=True)

        @pl.when(qi % 2 == 1)
        def _():
            step(qi - 1, 0, last=False)
            step(qi, 1, last=True)

    bounded = bound_ref[0] <= MAX_SAFE_BOUND
    pl.when(bounded)(bounded_path)
    pl.when(jnp.logical_not(bounded))(running_max_path)

    lam = (jnp.exp(jnp.sum(lq1_ref[...] * lk1_ref[...], keepdims=True))
           - jnp.exp(jnp.sum(lq2_ref[...] * lk2_ref[...], keepdims=True))
           + lambda_init)
    acc_a, acc_b = acc_sc[0], acc_sc[1]
    ot = (acc_a[:HEAD_WIDTH] / acc_a[HEAD_WIDTH:HEAD_WIDTH + 1]
          - lam * (acc_b[:HEAD_WIDTH] / acc_b[HEAD_WIDTH:HEAD_WIDTH + 1]))
    ot = ot * lax.rsqrt(jnp.mean(ot * ot, axis=0, keepdims=True) + EPS) * subg_ref[...]
    o_ref[0] = (ot * (1.0 - lambda_init)).T.astype(BF16)


def _attention(slopes, bound, q1, q2, k1, k2, v, lq1, lk1, lq2, lk2, subg, lambda_init):
    bsz, n_heads, n_tiles = q1.shape[:3]
    s, d = k1.shape[1], k1.shape[2]
    qspec = pl.BlockSpec((1, 1, 1, HEAD_WIDTH, TQ), lambda b, h, i: (b, h, i, 0, 0))
    kspec = pl.BlockSpec((1, s, HEAD_WIDTH), lambda b, h, i: (b, 0, h))
    vspec = pl.BlockSpec((1, 1, n_tiles, V_ROWS, TK), lambda b, h, i: (b, h, 0, 0, 0))
    small = lambda n: pl.BlockSpec((1, n), lambda b, h, i: (0, 0))
    return pl.pallas_call(
        functools.partial(_attn_kernel, lambda_init=lambda_init),
        grid=(bsz, n_heads, s // TQ),
        in_specs=[
            pl.BlockSpec(memory_space=pltpu.SMEM), pl.BlockSpec(memory_space=pltpu.SMEM),
            qspec, qspec, kspec, kspec, vspec,
            small(HEAD_DIM), small(HEAD_DIM), small(HEAD_DIM), small(HEAD_DIM),
            pl.BlockSpec((HEAD_WIDTH, 1), lambda b, h, i: (0, 0)),
        ],
        out_specs=pl.BlockSpec((1, TQ, HEAD_WIDTH), lambda b, h, i: (b, i, h)),
        out_shape=jax.ShapeDtypeStruct((bsz, s, d), BF16),
        scratch_shapes=[
            pltpu.VMEM((2, TK, TQ), F32),
            pltpu.VMEM((2, TK, TQ), BF16),
            pltpu.VMEM((2, 1, TQ), F32),
            pltpu.VMEM((1, TQ), F32),
            pltpu.VMEM((2, V_ROWS, TQ), F32),
        ],
        compiler_params=pltpu.CompilerParams(
            dimension_semantics=("arbitrary", "arbitrary", "arbitrary"),
            vmem_limit_bytes=VMEM_LIMIT),
        name="diff_attn",
    )(slopes, bound, q1, q2, k1, k2, v, lq1, lk1, lq2, lk2, subg)


def _row(v):
    return v.reshape(1, -1)


def _alibi_tables(n_heads):
    slopes = np.exp2(-8.0 * np.arange(1, n_heads + 1) / n_heads)
    assert np.all(np.log2(slopes) == np.round(np.log2(slopes))), "slopes must be exact in bf16"
    pos = np.arange(TM, dtype=np.float32) * np.float32(LOG2E)
    pieces = []
    for _ in range(N_POS_PIECES):
        piece = pos.astype(BF16).astype(np.float32)
        pieces.append(piece)
        pos = pos - piece
    cols = np.stack(pieces, axis=1)
    half = np.pad(cols, ((0, 0), (0, HEAD_DIM - N_POS_PIECES)))
    kext = np.tile(np.concatenate([half, half], axis=1), (1, n_heads))
    slot = (np.arange(HEAD_WIDTH) % HEAD_DIM) < N_POS_PIECES
    qext = (slopes[:, None] * slot[None, :]).reshape(-1, 1)
    return jnp.asarray(slopes, F32), jnp.asarray(kext, F32), jnp.asarray(qext, F32)


def _group_tables(d):
    group = np.arange(d) // HEAD_DIM
    gsum = (group[:, None] == np.arange(LANES)[None, :]).astype(np.float32)
    return jnp.asarray(gsum, BF16), jnp.asarray(gsum.T, BF16)


def kernel(x, ffn_norm, ffn_w_in, ffn_w_out, mix_norm, conv_w_in, conv_b_in, conv_dw, conv_dw_b, conv_ln_g, conv_ln_b, conv_w_out, conv_b_out, attn_w_qkv, attn_q_norm, attn_k_norm, attn_lq1, attn_lk1, attn_lq2, attn_lk2, attn_subln_g, attn_w_out):
    bsz, s, d = x.shape
    depth = ffn_norm.shape[0]
    n_heads = d // HEAD_WIDTH
    n_mixers = 2
    ffn_w_in = ffn_w_in.astype(BF16)
    ffn_w_out = ffn_w_out.astype(BF16)

    def ffn(x2, i, j, proj=None):
        return _ffn(x2, _row(ffn_norm[i, j]), ffn_w_in, ffn_w_out, (i, j), proj)

    x2 = x.reshape(bsz * s, d)
    for i in range(depth):
        x2 = ffn(x2, i, 0)
        j = i // n_mixers
        if i % n_mixers == 0:
            kpad = -(-CONV_WIDTH // 8) * 8
            dw3 = jnp.pad(conv_dw[j], ((0, kpad - CONV_WIDTH), (0, 0)))
            dw3 = dw3.reshape(kpad, d // LANES, LANES).transpose(1, 0, 2)
            x2 = _conv_mixer(
                x2.reshape(bsz, s, d), _row(mix_norm[i]), conv_w_in[j].astype(BF16), _row(conv_b_in[j]),
                dw3, _row(conv_dw_b[j]), _row(conv_ln_g[j]), _row(conv_ln_b[j]),
                conv_w_out[j].astype(BF16), _row(conv_b_out[j])).reshape(bsz * s, d)
            x2 = ffn(x2, i, 1)
        else:
            lambda_init = 0.8 - 0.6 * math.exp(-0.3 * i)
            gq = _row(jnp.tile(attn_q_norm[j], d // HEAD_DIM)) * (LOG2E / math.sqrt(HEAD_DIM))
            gk = _row(jnp.tile(attn_k_norm[j], d // HEAD_DIM))
            gsum, gexp = _group_tables(d)
            slopes, kext, qext = _alibi_tables(n_heads)
            q1, q2, k1, k2, v = _qkv(x2, _row(mix_norm[i]), attn_w_qkv[j].astype(BF16),
                                     gq, gk, gsum, gexp, kext, qext, bsz)
            bound = (HEAD_DIM * BOUND_MARGIN * jnp.max(jnp.abs(gq)) * jnp.max(jnp.abs(gk))).reshape(1)
            o = _attention(slopes, bound, q1, q2, k1.reshape(bsz, s, d), k2.reshape(bsz, s, d), v,
                           _row(attn_lq1[j]), _row(attn_lk1[j]), _row(attn_lq2[j]), _row(attn_lk2[j]),
                           attn_subln_g[j].reshape(-1, 1), lambda_init)
            x2 = ffn(x2, i, 1, proj=(o.reshape(bsz * s, d), attn_w_out[j].astype(BF16)))
    return x2.reshape(bsz, s, d)
```

```python
import functools
import math

import jax
import jax.numpy as jnp
import numpy as np
from jax import lax
from jax.experimental import pallas as pl
from jax.experimental.pallas import tpu as pltpu

EPS = 1e-6
CONV_WIDTH = 31
HEAD_DIM = 64
HEAD_WIDTH = 2 * HEAD_DIM
LANES = 128
MXU_COLS = 256
BF16_SUBLANES = 16
V_ROWS = HEAD_WIDTH + BF16_SUBLANES
N_POS_PIECES = 3
HALO = 32
NEG_BIG = -1e30
LOG2E = 1.4426950408889634
BOUND_MARGIN = 1.02
MAX_SAFE_BOUND = 1.0

TM = 512
FFN_TM = 1024
FFN_CHUNKS = 4
TK = 512
Q_SUB = 2
TQ = Q_SUB * TK
assert TM == TK
assert Q_SUB == 2
BOUNDED_GROUP = 4
CONV_ROWS = 64
VMEM_LIMIT = 56 * 1024 * 1024

F32 = jnp.float32
BF16 = jnp.bfloat16


def _rms(x, g):
    return x * lax.rsqrt(jnp.mean(x * x, axis=-1, keepdims=True) + EPS) * g


def _resident(shape, lead=()):
    index = tuple(lead) + (0,) * len(shape)
    return pl.BlockSpec((None,) * len(lead) + tuple(shape), lambda *_: index, pipeline_mode=pl.Buffered(1))


def _ffn_chunks(f, n_chunks):
    tiles = f // MXU_COLS
    assert tiles * MXU_COLS == f
    edges = [MXU_COLS * (tiles * c // n_chunks) for c in range(n_chunks + 1)]
    return list(zip(edges[:-1], edges[1:]))


def _ffn_kernel(*refs, n_chunks, project):
    if project:
        x_ref, a_ref, wp_ref, g_ref, win_ref, wo_ref, o_ref = refs
        x = x_ref[...] + jnp.dot(a_ref[...], wp_ref[...], preferred_element_type=F32)
    else:
        x_ref, g_ref, win_ref, wo_ref, o_ref = refs
        x = x_ref[...]
    f = wo_ref.shape[0]
    h = _rms(x, g_ref[...]).astype(BF16)
    y = jnp.zeros(x.shape, F32)
    for lo, hi in _ffn_chunks(f, n_chunks):
        gate = jnp.dot(h, win_ref[:, lo:hi], preferred_element_type=F32)
        up = jnp.dot(h, win_ref[:, f + lo:f + hi], preferred_element_type=F32)
        act = (gate * jax.nn.sigmoid(gate) * up).astype(BF16)
        y = y + jnp.dot(act, wo_ref[lo:hi, :], preferred_element_type=F32)
    o_ref[...] = x + 0.5 * y


def _ffn(x2, g, w_in, w_out, which, proj=None):
    t, d = x2.shape
    f = w_out.shape[-2]
    row = pl.BlockSpec((FFN_TM, d), lambda i: (i, 0))
    proj_specs = [] if proj is None else [row, _resident((d, d))]
    proj_args = () if proj is None else proj
    return pl.pallas_call(
        functools.partial(_ffn_kernel, n_chunks=FFN_CHUNKS, project=proj is not None),
        grid=(t // FFN_TM,),
        in_specs=[row, *proj_specs, _resident((1, d)),
                  _resident((d, 2 * f), which), _resident((f, d), which)],
        out_specs=row,
        out_shape=jax.ShapeDtypeStruct((t, d), F32),
        compiler_params=pltpu.CompilerParams(
            dimension_semantics=("arbitrary",), vmem_limit_bytes=VMEM_LIMIT),
        name="ffn" if proj is None else "proj_ffn",
    )(x2, *proj_args, g, w_in, w_out)


def _conv_kernel(x_ref, g_ref, win_ref, bin_ref, dw_ref, dwb_ref,
                 lng_ref, lnb_ref, wo_ref, bo_ref, o_ref, ubuf, cbuf):
    tm, d = x_ref.shape[1], x_ref.shape[2]
    n_lane_chunks = d // LANES

    @pl.when(pl.program_id(1) == 0)
    def _():
        ubuf[:, 0:HALO, :] = jnp.zeros((n_lane_chunks, HALO, LANES), F32)

    x = x_ref[0]
    h = _rms(x, g_ref[...]).astype(BF16)
    a = jnp.dot(h, win_ref[:, 0:d], preferred_element_type=F32) + bin_ref[:, 0:d]
    b = jnp.dot(h, win_ref[:, d:2 * d], preferred_element_type=F32) + bin_ref[:, d:2 * d]
    u = a * jax.nn.sigmoid(b)
    for c in range(n_lane_chunks):
        ubuf[c, HALO:HALO + tm, :] = u[:, c * LANES:(c + 1) * LANES]

    def lane_chunk(c, carry):
        for r0 in range(0, tm, CONV_ROWS):
            acc = jnp.zeros((CONV_ROWS, LANES), F32)
            for k in range(CONV_WIDTH):
                start = HALO + r0 - (CONV_WIDTH - 1) + k
                acc = acc + dw_ref[c, k:k + 1, :] * ubuf[c, start:start + CONV_ROWS, :]
            cbuf[c, r0:r0 + CONV_ROWS, :] = acc
        ubuf[c, 0:HALO, :] = ubuf[c, tm:tm + HALO, :]
        return carry

    lax.fori_loop(0, n_lane_chunks, lane_chunk, 0)

    y = jnp.concatenate([cbuf[c] for c in range(n_lane_chunks)], axis=1) + dwb_ref[...]
    mu = jnp.mean(y, axis=-1, keepdims=True)
    yc = y - mu
    var = jnp.mean(yc * yc, axis=-1, keepdims=True)
    z = yc * lax.rsqrt(var + EPS) * lng_ref[...] + lnb_ref[...]
    z = (z * jax.nn.sigmoid(z)).astype(BF16)
    m = jnp.dot(z, wo_ref[...], preferred_element_type=F32) + bo_ref[...]
    o_ref[0] = x + m


def _conv_mixer(x3, g, w_in, b_in, dw3, dwb, lng, lnb, wo, bo):
    bsz, s, d = x3.shape
    nl = d // LANES
    kpad = dw3.shape[1]
    return pl.pallas_call(
        _conv_kernel,
        grid=(bsz, s // TM),
        in_specs=[
            pl.BlockSpec((1, TM, d), lambda b, i: (b, i, 0)),
            _resident((1, d)),
            _resident((d, 2 * d)), _resident((1, 2 * d)),
            _resident((nl, kpad, LANES)),
            _resident((1, d)), _resident((1, d)), _resident((1, d)),
            _resident((d, d)), _resident((1, d)),
        ],
        out_specs=pl.BlockSpec((1, TM, d), lambda b, i: (b, i, 0)),
        out_shape=jax.ShapeDtypeStruct((bsz, s, d), F32),
        scratch_shapes=[
            pltpu.VMEM((nl, HALO + TM, LANES), F32),
            pltpu.VMEM((nl, TM, LANES), F32),
        ],
        compiler_params=pltpu.CompilerParams(
            dimension_semantics=("arbitrary", "arbitrary"), vmem_limit_bytes=VMEM_LIMIT),
        name="conv_mixer",
    )(x3, g, w_in, b_in, dw3, dwb, lng, lnb, wo, bo)


def _split_dot(v, w_ref):
    hi = v.astype(BF16)
    lo = (v - hi.astype(F32)).astype(BF16)
    return (jnp.dot(hi, w_ref[...], preferred_element_type=F32)
            + jnp.dot(lo, w_ref[...], preferred_element_type=F32))


def _group_rsqrt(v, gsum_ref, gexp_ref):
    ms = _split_dot(v * v, gsum_ref) * (1.0 / HEAD_DIM)
    return _split_dot(lax.rsqrt(ms + EPS), gexp_ref)


def _qkv_kernel(x_ref, g_ref, w_ref, gq_ref, gk_ref, gsum_ref, gexp_ref, kext_ref, qext_ref,
                q1_ref, q2_ref, k1_ref, k2_ref, v_ref):
    n_heads, tm = q1_ref.shape[1], q1_ref.shape[4]
    d = x_ref.shape[1]
    x = x_ref[...]
    h = _rms(x, g_ref[...]).astype(BF16)
    q = jnp.dot(h, w_ref[:, 0:d], preferred_element_type=F32)
    k = jnp.dot(h, w_ref[:, d:2 * d], preferred_element_type=F32)
    v = jnp.dot(h, w_ref[:, 2 * d:3 * d], preferred_element_type=F32)
    qn = q * _group_rsqrt(q, gsum_ref, gexp_ref) * gq_ref[...]
    kn = k * _group_rsqrt(k, gsum_ref, gexp_ref) * gk_ref[...]
    lane_first = (lax.broadcasted_iota(jnp.int32, kn.shape, 1) % HEAD_WIDTH) < HEAD_DIM
    kext = kext_ref[...]
    k1_ref[...] = jnp.where(lane_first, kn, kext).astype(BF16)
    k2_ref[...] = jnp.where(lane_first, kext, kn).astype(BF16)
    qt = qn.T.reshape(n_heads, HEAD_WIDTH, tm)
    row_first = lax.broadcasted_iota(jnp.int32, qt.shape, 1) < HEAD_DIM
    qext = jnp.broadcast_to(qext_ref[...].reshape(n_heads, HEAD_WIDTH, 1), qt.shape)
    q1_ref[0, :, 0] = jnp.where(row_first, qt, qext).astype(BF16)
    q2_ref[0, :, 0] = jnp.where(row_first, qext, qt).astype(BF16)
    v_ref[0, :, 0, 0:HEAD_WIDTH, :] = v.T.reshape(n_heads, HEAD_WIDTH, tm).astype(BF16)
    ones_row = lax.broadcasted_iota(jnp.int32, (n_heads, V_ROWS - HEAD_WIDTH, tm), 1) == 0
    v_ref[0, :, 0, HEAD_WIDTH:V_ROWS, :] = jnp.where(ones_row, 1.0, 0.0).astype(BF16)


def _qkv(x2, g, w_qkv, gq, gk, gsum, gexp, kext, qext, bsz):
    t, d = x2.shape
    n_heads = d // HEAD_WIDTH
    n_tiles = t // bsz // TM
    row = pl.BlockSpec((TM, d), lambda i: (i, 0))
    slab = lambda rows: pl.BlockSpec((1, n_heads, 1, rows, TM), lambda i: (i // n_tiles, 0, i % n_tiles, 0, 0))
    slab_shape = lambda rows: jax.ShapeDtypeStruct((bsz, n_heads, n_tiles, rows, TM), BF16)
    rows_shape = jax.ShapeDtypeStruct((t, d), BF16)
    return pl.pallas_call(
        _qkv_kernel,
        grid=(t // TM,),
        in_specs=[row, _resident((1, d)), _resident((d, 3 * d)),
                  _resident((1, d)), _resident((1, d)), _resident((d, LANES)), _resident((LANES, d)),
                  _resident((TM, d)), _resident((d, 1))],
        out_specs=[slab(HEAD_WIDTH), slab(HEAD_WIDTH), row, row, slab(V_ROWS)],
        out_shape=[slab_shape(HEAD_WIDTH), slab_shape(HEAD_WIDTH), rows_shape, rows_shape, slab_shape(V_ROWS)],
        compiler_params=pltpu.CompilerParams(
            dimension_semantics=("arbitrary",), vmem_limit_bytes=VMEM_LIMIT),
        name="qkv_proj",
    )(x2, g, w_qkv, gq, gk, gsum, gexp, kext, qext)


def _attn_kernel(slope_ref, bound_ref, q1_ref, q2_ref, k1_ref, k2_ref, v_ref, lq1_ref, lk1_ref, lq2_ref,
                 lk2_ref, subg_ref, o_ref, z_sc, p_sc, m_sc, a_sc, acc_sc, *, lambda_init):
    hd = pl.program_id(1)
    qi = pl.program_id(2)
    slope = slope_ref[hd] * LOG2E
    tile_step = slope * TK
    first_tile = qi * Q_SUB

    qs = tuple(jnp.concatenate([q_ref[0, 0, s] for s in range(Q_SUB)], axis=1)
               for q_ref in (q1_ref, q2_ref))
    ks = (k1_ref, k2_ref)

    def scores(j, c):
        kt = ks[c][0, pl.ds(pl.multiple_of(j * TK, TK), TK), :]
        return jnp.dot(kt, qs[c], preferred_element_type=F32)

    def causal(z, key_offset):
        krow = lax.broadcasted_iota(jnp.int32, (TK, TQ), 0)
        qcol = lax.broadcasted_iota(jnp.int32, (TK, TQ), 1)
        return jnp.where(krow + key_offset <= qcol, z, NEG_BIG)

    acc_sc[...] = jnp.zeros(acc_sc.shape, F32)

    def bounded_tiles(j0, n_tiles, with_diagonal):
        qoff = lax.broadcasted_iota(jnp.int32, (1, TQ), 1).astype(F32)
        ref = bound_ref[0] + slope * qoff
        n_wide = n_tiles - 1 if with_diagonal else n_tiles
        for c in range(2):
            ps = []
            for t in range(n_wide):
                z = scores(j0 + t, c)
                if with_diagonal and t == n_wide - 1:
                    z = causal(z, 0)
                off = tile_step * (j0 + t - first_tile).astype(F32) - ref
                ps.append(jnp.exp2(z + off).astype(BF16))
            vt = jnp.concatenate([v_ref[0, 0, j0 + t] for t in range(n_wide)], axis=1)
            acc_sc[c] += jnp.dot(vt, jnp.concatenate(ps, axis=0), preferred_element_type=F32)
            if with_diagonal:
                jl = j0 + n_wide
                kt = ks[c][0, pl.ds(pl.multiple_of(jl * TK, TK), TK), :]
                z = jnp.dot(kt, qs[c][:, TK:], preferred_element_type=F32)
                krow = lax.broadcasted_iota(jnp.int32, (TK, TK), 0)
                qcol = lax.broadcasted_iota(jnp.int32, (TK, TK), 1)
                z = jnp.where(krow <= qcol, z, NEG_BIG)
                off = tile_step * (jl - first_tile).astype(F32) - ref[:, TK:]
                p = jnp.exp2(z + off).astype(BF16)
                acc_sc[c, :, TK:] += jnp.dot(v_ref[0, 0, jl], p, preferred_element_type=F32)

    def bounded_path():
        def group(t, carry):
            bounded_tiles(BOUNDED_GROUP * t, BOUNDED_GROUP, with_diagonal=False)
            return carry

        lax.fori_loop(0, first_tile // BOUNDED_GROUP, group, 0)
        for rest in range(0, BOUNDED_GROUP, Q_SUB):
            pl.when(first_tile % BOUNDED_GROUP == rest)(
                functools.partial(bounded_tiles, first_tile - rest, rest + Q_SUB, with_diagonal=True))

    def softmax(z, j, c, key_offset):
        if key_offset is not None:
            z = causal(z, key_offset)
        shift = tile_step * (j - first_tile).astype(F32)
        m_old = m_sc[c]
        m_new = jnp.maximum(m_old, jnp.max(z, axis=0, keepdims=True) + shift)
        m_sc[c] = m_new
        return jnp.exp2(z - (m_new - shift)).astype(BF16), jnp.exp2(m_old - m_new)

    def accumulate(c, j, p, alpha):
        acc_sc[c] = alpha * acc_sc[c] + jnp.dot(v_ref[0, 0, j], p, preferred_element_type=F32)

    def step(j, par, key_offset=None, last=False):
        zb = scores(j, 1)
        pa, alpha_a = softmax(z_sc[par], j, 0, key_offset)
        accumulate(1, jnp.maximum(j - 1, 0), p_sc[1 - par], a_sc[...])
        if not last:
            z_sc[1 - par] = scores(j + 1, 0)
        pb, alpha_b = softmax(zb, j, 1, key_offset)
        accumulate(0, j, pa, alpha_a)
        if last:
            accumulate(1, j, pb, alpha_b)
        else:
            p_sc[par] = pb
            a_sc[...] = alpha_b

    def running_max_path():
        m_sc[...] = jnp.full(m_sc.shape, NEG_BIG, F32)
        a_sc[...] = jnp.ones(a_sc.shape, F32)
        p_sc[1] = jnp.zeros(p_sc.shape[1:], BF16)
        z_sc[0] = scores(0, 0)

        def pair(t, carry):
            step(2 * t, 0)
            step(2 * t + 1, 1)
            return carry

        lax.fori_loop(0, first_tile // 2, pair, 0)
        for s in range(Q_SUB):
            step(first_tile + s, s % 2, key_offset=s * TK, last=s == Q_SUB - 1)

    bounded = bound_ref[0] <= MAX_SAFE_BOUND
    pl.when(bounded)(bounded_path)
    pl.when(jnp.logical_not(bounded))(running_max_path)

    lam = (jnp.exp(jnp.sum(lq1_ref[...] * lk1_ref[...], keepdims=True))
           - jnp.exp(jnp.sum(lq2_ref[...] * lk2_ref[...], keepdims=True))
           + lambda_init)
    acc_a, acc_b = acc_sc[0], acc_sc[1]
    ot = (acc_a[:HEAD_WIDTH] / acc_a[HEAD_WIDTH:HEAD_WIDTH + 1]
          - lam * (acc_b[:HEAD_WIDTH] / acc_b[HEAD_WIDTH:HEAD_WIDTH + 1]))
    ot = ot * lax.rsqrt(jnp.mean(ot * ot, axis=0, keepdims=True) + EPS) * subg_ref[...]
    o_ref[0] = (ot * (1.0 - lambda_init)).T.astype(BF16)


def _attention(slopes, bound, q1, q2, k1, k2, v, lq1, lk1, lq2, lk2, subg, lambda_init):
    bsz, n_heads, n_tiles = q1.shape[:3]
    s, d = k1.shape[1], k1.shape[2]
    qspec = pl.BlockSpec((1, 1, Q_SUB, HEAD_WIDTH, TK), lambda b, h, i: (b, h, i, 0, 0))
    kspec = pl.BlockSpec((1, s, HEAD_WIDTH), lambda b, h, i: (b, 0, h))
    vspec = pl.BlockSpec((1, 1, n_tiles, V_ROWS, TK), lambda b, h, i: (b, h, 0, 0, 0))
    small = lambda n: pl.BlockSpec((1, n), lambda b, h, i: (0, 0))
    return pl.pallas_call(
        functools.partial(_attn_kernel, lambda_init=lambda_init),
        grid=(bsz, n_heads, s // TQ),
        in_specs=[
            pl.BlockSpec(memory_space=pltpu.SMEM), pl.BlockSpec(memory_space=pltpu.SMEM),
            qspec, qspec, kspec, kspec, vspec,
            small(HEAD_DIM), small(HEAD_DIM), small(HEAD_DIM), small(HEAD_DIM),
            pl.BlockSpec((HEAD_WIDTH, 1), lambda b, h, i: (0, 0)),
        ],
        out_specs=pl.BlockSpec((1, TQ, HEAD_WIDTH), lambda b, h, i: (b, i, h)),
        out_shape=jax.ShapeDtypeStruct((bsz, s, d), BF16),
        scratch_shapes=[
            pltpu.VMEM((2, TK, TQ), F32),
            pltpu.VMEM((2, TK, TQ), BF16),
            pltpu.VMEM((2, 1, TQ), F32),
            pltpu.VMEM((1, TQ), F32),
            pltpu.VMEM((2, V_ROWS, TQ), F32),
        ],
        compiler_params=pltpu.CompilerParams(
            dimension_semantics=("arbitrary", "arbitrary", "arbitrary"),
            vmem_limit_bytes=VMEM_LIMIT),
        name="diff_attn",
    )(slopes, bound, q1, q2, k1, k2, v, lq1, lk1, lq2, lk2, subg)


def _row(v):
    return v.reshape(1, -1)


def _alibi_tables(n_heads):
    slopes = np.exp2(-8.0 * np.arange(1, n_heads + 1) / n_heads)
    assert np.all(np.log2(slopes) == np.round(np.log2(slopes))), "slopes must be exact in bf16"
    pos = np.arange(TM, dtype=np.float32) * np.float32(LOG2E)
    pieces = []
    for _ in range(N_POS_PIECES):
        piece = pos.astype(BF16).astype(np.float32)
        pieces.append(piece)
        pos = pos - piece
    cols = np.stack(pieces, axis=1)
    half = np.pad(cols, ((0, 0), (0, HEAD_DIM - N_POS_PIECES)))
    kext = np.tile(np.concatenate([half, half], axis=1), (1, n_heads))
    slot = (np.arange(HEAD_WIDTH) % HEAD_DIM) < N_POS_PIECES
    qext = (slopes[:, None] * slot[None, :]).reshape(-1, 1)
    return jnp.asarray(slopes, F32), jnp.asarray(kext, F32), jnp.asarray(qext, F32)


def _group_tables(d):
    group = np.arange(d) // HEAD_DIM
    gsum = (group[:, None] == np.arange(LANES)[None, :]).astype(np.float32)
    return jnp.asarray(gsum, BF16), jnp.asarray(gsum.T, BF16)


def kernel(x, ffn_norm, ffn_w_in, ffn_w_out, mix_norm, conv_w_in, conv_b_in, conv_dw, conv_dw_b, conv_ln_g, conv_ln_b, conv_w_out, conv_b_out, attn_w_qkv, attn_q_norm, attn_k_norm, attn_lq1, attn_lk1, attn_lq2, attn_lk2, attn_subln_g, attn_w_out):
    bsz, s, d = x.shape
    depth = ffn_norm.shape[0]
    n_heads = d // HEAD_WIDTH
    n_mixers = 2
    ffn_w_in = ffn_w_in.astype(BF16)
    ffn_w_out = ffn_w_out.astype(BF16)

    def ffn(x2, i, j, proj=None):
        return _ffn(x2, _row(ffn_norm[i, j]), ffn_w_in, ffn_w_out, (i, j), proj)

    x2 = x.reshape(bsz * s, d)
    for i in range(depth):
        x2 = ffn(x2, i, 0)
        j = i // n_mixers
        if i % n_mixers == 0:
            kpad = -(-CONV_WIDTH // 8) * 8
            dw3 = jnp.pad(conv_dw[j], ((0, kpad - CONV_WIDTH), (0, 0)))
            dw3 = dw3.reshape(kpad, d // LANES, LANES).transpose(1, 0, 2)
            x2 = _conv_mixer(
                x2.reshape(bsz, s, d), _row(mix_norm[i]), conv_w_in[j].astype(BF16), _row(conv_b_in[j]),
                dw3, _row(conv_dw_b[j]), _row(conv_ln_g[j]), _row(conv_ln_b[j]),
                conv_w_out[j].astype(BF16), _row(conv_b_out[j])).reshape(bsz * s, d)
            x2 = ffn(x2, i, 1)
        else:
            lambda_init = 0.8 - 0.6 * math.exp(-0.3 * i)
            gq = _row(jnp.tile(attn_q_norm[j], d // HEAD_DIM)) * (LOG2E / math.sqrt(HEAD_DIM))
            gk = _row(jnp.tile(attn_k_norm[j], d // HEAD_DIM))
            gsum, gexp = _group_tables(d)
            slopes, kext, qext = _alibi_tables(n_heads)
            q1, q2, k1, k2, v = _qkv(x2, _row(mix_norm[i]), attn_w_qkv[j].astype(BF16),
                                     gq, gk, gsum, gexp, kext, qext, bsz)
            bound = (HEAD_DIM * BOUND_MARGIN * jnp.max(jnp.abs(gq)) * jnp.max(jnp.abs(gk))).reshape(1)
            o = _attention(slopes, bound, q1, q2, k1.reshape(bsz, s, d), k2.reshape(bsz, s, d), v,
                           _row(attn_lq1[j]), _row(attn_lk1[j]), _row(attn_lq2[j]), _row(attn_lk2[j]),
                           attn_subln_g[j].reshape(-1, 1), lambda_init)
            x2 = ffn(x2, i, 1, proj=(o.reshape(bsz * s, d), attn_w_out[j].astype(BF16)))
    return x2.reshape(bsz, s, d)
```

```python
import functools
import math

import jax
import jax.numpy as jnp
import numpy as np
from jax import lax
from jax.experimental import pallas as pl
from jax.experimental.pallas import tpu as pltpu

EPS = 1e-6
CONV_WIDTH = 31
HEAD_DIM = 64
HEAD_WIDTH = 2 * HEAD_DIM
LANES = 128
MXU_COLS = 256
BF16_SUBLANES = 16
V_ROWS = HEAD_WIDTH + BF16_SUBLANES
N_POS_PIECES = 3
HALO = 32
NEG_BIG = -1e30
LOG2E = 1.4426950408889634
BOUND_MARGIN = 1.02
MAX_SAFE_BOUND = 40.0

TM = 512
FFN_TM = 1024
FFN_CHUNKS = 4
TK = 512
Q_SUB = 2
TQ = Q_SUB * TK
assert TM == TK
assert Q_SUB == 2
BOUNDED_GROUP = 4
CONV_ROWS = 64
VMEM_LIMIT = 56 * 1024 * 1024

F32 = jnp.float32
BF16 = jnp.bfloat16


def _rms(x, g):
    return x * lax.rsqrt(jnp.mean(x * x, axis=-1, keepdims=True) + EPS) * g


def _resident(shape, lead=()):
    index = tuple(lead) + (0,) * len(shape)
    return pl.BlockSpec((None,) * len(lead) + tuple(shape), lambda *_: index, pipeline_mode=pl.Buffered(1))


def _ffn_chunks(f, n_chunks):
    tiles = f // MXU_COLS
    assert tiles * MXU_COLS == f
    edges = [MXU_COLS * (tiles * c // n_chunks) for c in range(n_chunks + 1)]
    return list(zip(edges[:-1], edges[1:]))


def _ffn_kernel(*refs, n_chunks, project):
    if project:
        x_ref, a_ref, wp_ref, g_ref, win_ref, wo_ref, o_ref = refs
        x = x_ref[...] + jnp.dot(a_ref[...], wp_ref[...], preferred_element_type=F32)
    else:
        x_ref, g_ref, win_ref, wo_ref, o_ref = refs
        x = x_ref[...]
    f = wo_ref.shape[0]
    h = _rms(x, g_ref[...]).astype(BF16)
    y = jnp.zeros(x.shape, F32)
    for lo, hi in _ffn_chunks(f, n_chunks):
        gate = jnp.dot(h, win_ref[:, lo:hi], preferred_element_type=F32)
        up = jnp.dot(h, win_ref[:, f + lo:f + hi], preferred_element_type=F32)
        act = (gate * jax.nn.sigmoid(gate) * up).astype(BF16)
        y = y + jnp.dot(act, wo_ref[lo:hi, :], preferred_element_type=F32)
    o_ref[...] = x + 0.5 * y


def _ffn(x2, g, w_in, w_out, which, proj=None):
    t, d = x2.shape
    f = w_out.shape[-2]
    row = pl.BlockSpec((FFN_TM, d), lambda i: (i, 0))
    proj_specs = [] if proj is None else [row, _resident((d, d))]
    proj_args = () if proj is None else proj
    return pl.pallas_call(
        functools.partial(_ffn_kernel, n_chunks=FFN_CHUNKS, project=proj is not None),
        grid=(t // FFN_TM,),
        in_specs=[row, *proj_specs, _resident((1, d)),
                  _resident((d, 2 * f), which), _resident((f, d), which)],
        out_specs=row,
        out_shape=jax.ShapeDtypeStruct((t, d), F32),
        compiler_params=pltpu.CompilerParams(
            dimension_semantics=("arbitrary",), vmem_limit_bytes=VMEM_LIMIT),
        name="ffn" if proj is None else "proj_ffn",
    )(x2, *proj_args, g, w_in, w_out)


def _conv_kernel(x_ref, g_ref, win_ref, bin_ref, dw_ref, dwb_ref,
                 lng_ref, lnb_ref, wo_ref, bo_ref, o_ref, ubuf, cbuf):
    tm, d = x_ref.shape[1], x_ref.shape[2]
    n_lane_chunks = d // LANES

    @pl.when(pl.program_id(1) == 0)
    def _():
        ubuf[:, 0:HALO, :] = jnp.zeros((n_lane_chunks, HALO, LANES), F32)

    x = x_ref[0]
    h = _rms(x, g_ref[...]).astype(BF16)
    a = jnp.dot(h, win_ref[:, 0:d], preferred_element_type=F32) + bin_ref[:, 0:d]
    b = jnp.dot(h, win_ref[:, d:2 * d], preferred_element_type=F32) + bin_ref[:, d:2 * d]
    u = a * jax.nn.sigmoid(b)
    for c in range(n_lane_chunks):
        ubuf[c, HALO:HALO + tm, :] = u[:, c * LANES:(c + 1) * LANES]

    def lane_chunk(c, carry):
        for r0 in range(0, tm, CONV_ROWS):
            acc = jnp.zeros((CONV_ROWS, LANES), F32)
            for k in range(CONV_WIDTH):
                start = HALO + r0 - (CONV_WIDTH - 1) + k
                acc = acc + dw_ref[c, k:k + 1, :] * ubuf[c, start:start + CONV_ROWS, :]
            cbuf[c, r0:r0 + CONV_ROWS, :] = acc
        ubuf[c, 0:HALO, :] = ubuf[c, tm:tm + HALO, :]
        return carry

    lax.fori_loop(0, n_lane_chunks, lane_chunk, 0)

    y = jnp.concatenate([cbuf[c] for c in range(n_lane_chunks)], axis=1) + dwb_ref[...]
    mu = jnp.mean(y, axis=-1, keepdims=True)
    yc = y - mu
    var = jnp.mean(yc * yc, axis=-1, keepdims=True)
    z = yc * lax.rsqrt(var + EPS) * lng_ref[...] + lnb_ref[...]
    z = (z * jax.nn.sigmoid(z)).astype(BF16)
    m = jnp.dot(z, wo_ref[...], preferred_element_type=F32) + bo_ref[...]
    o_ref[0] = x + m


def _conv_mixer(x3, g, w_in, b_in, dw3, dwb, lng, lnb, wo, bo):
    bsz, s, d = x3.shape
    nl = d // LANES
    kpad = dw3.shape[1]
    return pl.pallas_call(
        _conv_kernel,
        grid=(bsz, s // TM),
        in_specs=[
            pl.BlockSpec((1, TM, d), lambda b, i: (b, i, 0)),
            _resident((1, d)),
            _resident((d, 2 * d)), _resident((1, 2 * d)),
            _resident((nl, kpad, LANES)),
            _resident((1, d)), _resident((1, d)), _resident((1, d)),
            _resident((d, d)), _resident((1, d)),
        ],
        out_specs=pl.BlockSpec((1, TM, d), lambda b, i: (b, i, 0)),
        out_shape=jax.ShapeDtypeStruct((bsz, s, d), F32),
        scratch_shapes=[
            pltpu.VMEM((nl, HALO + TM, LANES), F32),
            pltpu.VMEM((nl, TM, LANES), F32),
        ],
        compiler_params=pltpu.CompilerParams(
            dimension_semantics=("arbitrary", "arbitrary"), vmem_limit_bytes=VMEM_LIMIT),
        name="conv_mixer",
    )(x3, g, w_in, b_in, dw3, dwb, lng, lnb, wo, bo)


def _split_dot(v, w_ref):
    hi = v.astype(BF16)
    lo = (v - hi.astype(F32)).astype(BF16)
    return (jnp.dot(hi, w_ref[...], preferred_element_type=F32)
            + jnp.dot(lo, w_ref[...], preferred_element_type=F32))


def _group_rsqrt(v, gsum_ref, gexp_ref):
    ms = _split_dot(v * v, gsum_ref) * (1.0 / HEAD_DIM)
    return _split_dot(lax.rsqrt(ms + EPS), gexp_ref)


def _qkv_kernel(x_ref, g_ref, w_ref, gq_ref, gk_ref, gsum_ref, gexp_ref, kext_ref, qext_ref,
                q1_ref, q2_ref, k1_ref, k2_ref, v_ref):
    n_heads, tm = q1_ref.shape[1], q1_ref.shape[4]
    d = x_ref.shape[1]
    x = x_ref[...]
    h = _rms(x, g_ref[...]).astype(BF16)
    q = jnp.dot(h, w_ref[:, 0:d], preferred_element_type=F32)
    k = jnp.dot(h, w_ref[:, d:2 * d], preferred_element_type=F32)
    v = jnp.dot(h, w_ref[:, 2 * d:3 * d], preferred_element_type=F32)
    qn = q * _group_rsqrt(q, gsum_ref, gexp_ref) * gq_ref[...]
    kn = k * _group_rsqrt(k, gsum_ref, gexp_ref) * gk_ref[...]
    lane_first = (lax.broadcasted_iota(jnp.int32, kn.shape, 1) % HEAD_WIDTH) < HEAD_DIM
    kext = kext_ref[...]
    k1_ref[...] = jnp.where(lane_first, kn, kext).astype(BF16)
    k2_ref[...] = jnp.where(lane_first, kext, kn).astype(BF16)
    qt = qn.T.reshape(n_heads, HEAD_WIDTH, tm)
    row_first = lax.broadcasted_iota(jnp.int32, qt.shape, 1) < HEAD_DIM
    qext = jnp.broadcast_to(qext_ref[...].reshape(n_heads, HEAD_WIDTH, 1), qt.shape)
    q1_ref[0, :, 0] = jnp.where(row_first, qt, qext).astype(BF16)
    q2_ref[0, :, 0] = jnp.where(row_first, qext, qt).astype(BF16)
    v_ref[0, :, 0, 0:HEAD_WIDTH, :] = v.T.reshape(n_heads, HEAD_WIDTH, tm).astype(BF16)
    ones_row = lax.broadcasted_iota(jnp.int32, (n_heads, V_ROWS - HEAD_WIDTH, tm), 1) == 0
    v_ref[0, :, 0, HEAD_WIDTH:V_ROWS, :] = jnp.where(ones_row, 1.0, 0.0).astype(BF16)


def _qkv(x2, g, w_qkv, gq, gk, gsum, gexp, kext, qext, bsz):
    t, d = x2.shape
    n_heads = d // HEAD_WIDTH
    n_tiles = t // bsz // TM
    row = pl.BlockSpec((TM, d), lambda i: (i, 0))
    slab = lambda rows: pl.BlockSpec((1, n_heads, 1, rows, TM), lambda i: (i // n_tiles, 0, i % n_tiles, 0, 0))
    slab_shape = lambda rows: jax.ShapeDtypeStruct((bsz, n_heads, n_tiles, rows, TM), BF16)
    rows_shape = jax.ShapeDtypeStruct((t, d), BF16)
    return pl.pallas_call(
        _qkv_kernel,
        grid=(t // TM,),
        in_specs=[row, _resident((1, d)), _resident((d, 3 * d)),
                  _resident((1, d)), _resident((1, d)), _resident((d, LANES)), _resident((LANES, d)),
                  _resident((TM, d)), _resident((d, 1))],
        out_specs=[slab(HEAD_WIDTH), slab(HEAD_WIDTH), row, row, slab(V_ROWS)],
        out_shape=[slab_shape(HEAD_WIDTH), slab_shape(HEAD_WIDTH), rows_shape, rows_shape, slab_shape(V_ROWS)],
        compiler_params=pltpu.CompilerParams(
            dimension_semantics=("arbitrary",), vmem_limit_bytes=VMEM_LIMIT),
        name="qkv_proj",
    )(x2, g, w_qkv, gq, gk, gsum, gexp, kext, qext)


def _attn_kernel(slope_ref, bound_ref, q1_ref, q2_ref, k1_ref, k2_ref, v_ref, lq1_ref, lk1_ref, lq2_ref,
                 lk2_ref, subg_ref, o_ref, z_sc, p_sc, m_sc, a_sc, acc_sc, *, lambda_init):
    hd = pl.program_id(1)
    qi = pl.program_id(2)
    slope = slope_ref[hd] * LOG2E
    tile_step = slope * TK
    first_tile = qi * Q_SUB

    qs = tuple(jnp.concatenate([q_ref[0, 0, s] for s in range(Q_SUB)], axis=1)
               for q_ref in (q1_ref, q2_ref))
    ks = (k1_ref, k2_ref)

    def scores(j, c):
        kt = ks[c][0, pl.ds(pl.multiple_of(j * TK, TK), TK), :]
        return jnp.dot(kt, qs[c], preferred_element_type=F32)

    def causal(z, key_offset):
        krow = lax.broadcasted_iota(jnp.int32, (TK, TQ), 0)
        qcol = lax.broadcasted_iota(jnp.int32, (TK, TQ), 1)
        return jnp.where(krow + key_offset <= qcol, z, NEG_BIG)

    acc_sc[...] = jnp.zeros(acc_sc.shape, F32)

    def bounded_tiles(j0, n_tiles, with_diagonal):
        qoff = lax.broadcasted_iota(jnp.int32, (1, TQ), 1).astype(F32)
        ref = bound_ref[0] + slope * qoff
        n_wide = n_tiles - 1 if with_diagonal else n_tiles
        for c in range(2):
            ps = []
            for t in range(n_wide):
                z = scores(j0 + t, c)
                if with_diagonal and t == n_wide - 1:
                    z = causal(z, 0)
                off = tile_step * (j0 + t - first_tile).astype(F32) - ref
                ps.append(jnp.exp2(z + off).astype(BF16))
            vt = jnp.concatenate([v_ref[0, 0, j0 + t] for t in range(n_wide)], axis=1)
            acc_sc[c] += jnp.dot(vt, jnp.concatenate(ps, axis=0), preferred_element_type=F32)
            if with_diagonal:
                jl = j0 + n_wide
                kt = ks[c][0, pl.ds(pl.multiple_of(jl * TK, TK), TK), :]
                z = jnp.dot(kt, qs[c][:, TK:], preferred_element_type=F32)
                krow = lax.broadcasted_iota(jnp.int32, (TK, TK), 0)
                qcol = lax.broadcasted_iota(jnp.int32, (TK, TK), 1)
                z = jnp.where(krow <= qcol, z, NEG_BIG)
                off = tile_step * (jl - first_tile).astype(F32) - ref[:, TK:]
                p = jnp.exp2(z + off).astype(BF16)
                acc_sc[c, :, TK:] += jnp.dot(v_ref[0, 0, jl], p, preferred_element_type=F32)

    def bounded_path():
        def group(t, carry):
            bounded_tiles(BOUNDED_GROUP * t, BOUNDED_GROUP, with_diagonal=False)
            return carry

        lax.fori_loop(0, first_tile // BOUNDED_GROUP, group, 0)
        for rest in range(0, BOUNDED_GROUP, Q_SUB):
            pl.when(first_tile % BOUNDED_GROUP == rest)(
                functools.partial(bounded_tiles, first_tile - rest, rest + Q_SUB, with_diagonal=True))

    def softmax(z, j, c, key_offset):
        if key_offset is not None:
            z = causal(z, key_offset)
        shift = tile_step * (j - first_tile).astype(F32)
        m_old = m_sc[c]
        m_new = jnp.maximum(m_old, jnp.max(z, axis=0, keepdims=True) + shift)
        m_sc[c] = m_new
        return jnp.exp2(z - (m_new - shift)).astype(BF16), jnp.exp2(m_old - m_new)

    def accumulate(c, j, p, alpha):
        acc_sc[c] = alpha * acc_sc[c] + jnp.dot(v_ref[0, 0, j], p, preferred_element_type=F32)

    def step(j, par, key_offset=None, last=False):
        zb = scores(j, 1)
        pa, alpha_a = softmax(z_sc[par], j, 0, key_offset)
        accumulate(1, jnp.maximum(j - 1, 0), p_sc[1 - par], a_sc[...])
        if not last:
            z_sc[1 - par] = scores(j + 1, 0)
        pb, alpha_b = softmax(zb, j, 1, key_offset)
        accumulate(0, j, pa, alpha_a)
        if last:
            accumulate(1, j, pb, alpha_b)
        else:
            p_sc[par] = pb
            a_sc[...] = alpha_b

    def running_max_path():
        m_sc[...] = jnp.full(m_sc.shape, NEG_BIG, F32)
        a_sc[...] = jnp.ones(a_sc.shape, F32)
        p_sc[1] = jnp.zeros(p_sc.shape[1:], BF16)
        z_sc[0] = scores(0, 0)

        def pair(t, carry):
            step(2 * t, 0)
            step(2 * t + 1, 1)
            return carry

        lax.fori_loop(0, first_tile // 2, pair, 0)
        for s in range(Q_SUB):
            step(first_tile + s, s % 2, key_offset=s * TK, last=s == Q_SUB - 1)

    bounded = bound_ref[0] <= MAX_SAFE_BOUND
    pl.when(bounded)(bounded_path)
    pl.when(jnp.logical_not(bounded))(running_max_path)

    lam = (jnp.exp(jnp.sum(lq1_ref[...] * lk1_ref[...], keepdims=True))
           - jnp.exp(jnp.sum(lq2_ref[...] * lk2_ref[...], keepdims=True))
           + lambda_init)
    acc_a, acc_b = acc_sc[0], acc_sc[1]
    ot = (acc_a[:HEAD_WIDTH] / acc_a[HEAD_WIDTH:HEAD_WIDTH + 1]
          - lam * (acc_b[:HEAD_WIDTH] / acc_b[HEAD_WIDTH:HEAD_WIDTH + 1]))
    ot = ot * lax.rsqrt(jnp.mean(ot * ot, axis=0, keepdims=True) + EPS) * subg_ref[...]
    o_ref[0] = (ot * (1.0 - lambda_init)).T.astype(BF16)


def _attention(slopes, bound, q1, q2, k1, k2, v, lq1, lk1, lq2, lk2, subg, lambda_init):
    bsz, n_heads, n_tiles = q1.shape[:3]
    s, d = k1.shape[1], k1.shape[2]
    qspec = pl.BlockSpec((1, 1, Q_SUB, HEAD_WIDTH, TK), lambda b, h, i: (b, h, i, 0, 0))
    kspec = pl.BlockSpec((1, s, HEAD_WIDTH), lambda b, h, i: (b, 0, h))
    vspec = pl.BlockSpec((1, 1, n_tiles, V_ROWS, TK), lambda b, h, i: (b, h, 0, 0, 0))
    small = lambda n: pl.BlockSpec((1, n), lambda b, h, i: (0, 0))
    return pl.pallas_call(
        functools.partial(_attn_kernel, lambda_init=lambda_init),
        grid=(bsz, n_heads, s // TQ),
        in_specs=[
            pl.BlockSpec(memory_space=pltpu.SMEM), pl.BlockSpec(memory_space=pltpu.SMEM),
            qspec, qspec, kspec, kspec, vspec,
            small(HEAD_DIM), small(HEAD_DIM), small(HEAD_DIM), small(HEAD_DIM),
            pl.BlockSpec((HEAD_WIDTH, 1), lambda b, h, i: (0, 0)),
        ],
        out_specs=pl.BlockSpec((1, TQ, HEAD_WIDTH), lambda b, h, i: (b, i, h)),
        out_shape=jax.ShapeDtypeStruct((bsz, s, d), BF16),
        scratch_shapes=[
            pltpu.VMEM((2, TK, TQ), F32),
            pltpu.VMEM((2, TK, TQ), BF16),
            pltpu.VMEM((2, 1, TQ), F32),
            pltpu.VMEM((1, TQ), F32),
            pltpu.VMEM((2, V_ROWS, TQ), F32),
        ],
        compiler_params=pltpu.CompilerParams(
            dimension_semantics=("arbitrary", "arbitrary", "arbitrary"),
            vmem_limit_bytes=VMEM_LIMIT),
        name="diff_attn",
    )(slopes, bound, q1, q2, k1, k2, v, lq1, lk1, lq2, lk2, subg)


def _row(v):
    return v.reshape(1, -1)


def _alibi_tables(n_heads):
    slopes = np.exp2(-8.0 * np.arange(1, n_heads + 1) / n_heads)
    assert np.all(np.log2(slopes) == np.round(np.log2(slopes))), "slopes must be exact in bf16"
    pos = np.arange(TM, dtype=np.float32) * np.float32(LOG2E)
    pieces = []
    for _ in range(N_POS_PIECES):
        piece = pos.astype(BF16).astype(np.float32)
        pieces.append(piece)
        pos = pos - piece
    cols = np.stack(pieces, axis=1)
    half = np.pad(cols, ((0, 0), (0, HEAD_DIM - N_POS_PIECES)))
    kext = np.tile(np.concatenate([half, half], axis=1), (1, n_heads))
    slot = (np.arange(HEAD_WIDTH) % HEAD_DIM) < N_POS_PIECES
    qext = (slopes[:, None] * slot[None, :]).reshape(-1, 1)
    return jnp.asarray(slopes, F32), jnp.asarray(kext, F32), jnp.asarray(qext, F32)


def _group_tables(d):
    group = np.arange(d) // HEAD_DIM
    gsum = (group[:, None] == np.arange(LANES)[None, :]).astype(np.float32)
    return jnp.asarray(gsum, BF16), jnp.asarray(gsum.T, BF16)


def kernel(x, ffn_norm, ffn_w_in, ffn_w_out, mix_norm, conv_w_in, conv_b_in, conv_dw, conv_dw_b, conv_ln_g, conv_ln_b, conv_w_out, conv_b_out, attn_w_qkv, attn_q_norm, attn_k_norm, attn_lq1, attn_lk1, attn_lq2, attn_lk2, attn_subln_g, attn_w_out):
    bsz, s, d = x.shape
    depth = ffn_norm.shape[0]
    n_heads = d // HEAD_WIDTH
    n_mixers = 2
    ffn_w_in = ffn_w_in.astype(BF16)
    ffn_w_out = ffn_w_out.astype(BF16)

    def ffn(x2, i, j, proj=None):
        return _ffn(x2, _row(ffn_norm[i, j]), ffn_w_in, ffn_w_out, (i, j), proj)

    x2 = x.reshape(bsz * s, d)
    for i in range(depth):
        x2 = ffn(x2, i, 0)
        j = i // n_mixers
        if i % n_mixers == 0:
            kpad = -(-CONV_WIDTH // 8) * 8
            dw3 = jnp.pad(conv_dw[j], ((0, kpad - CONV_WIDTH), (0, 0)))
            dw3 = dw3.reshape(kpad, d // LANES, LANES).transpose(1, 0, 2)
            x2 = _conv_mixer(
                x2.reshape(bsz, s, d), _row(mix_norm[i]), conv_w_in[j].astype(BF16), _row(conv_b_in[j]),
                dw3, _row(conv_dw_b[j]), _row(conv_ln_g[j]), _row(conv_ln_b[j]),
                conv_w_out[j].astype(BF16), _row(conv_b_out[j])).reshape(bsz * s, d)
            x2 = ffn(x2, i, 1)
        else:
            lambda_init = 0.8 - 0.6 * math.exp(-0.3 * i)
            gq = _row(jnp.tile(attn_q_norm[j], d // HEAD_DIM)) * (LOG2E / math.sqrt(HEAD_DIM))
            gk = _row(jnp.tile(attn_k_norm[j], d // HEAD_DIM))
            gsum, gexp = _group_tables(d)
            slopes, kext, qext = _alibi_tables(n_heads)
            q1, q2, k1, k2, v = _qkv(x2, _row(mix_norm[i]), attn_w_qkv[j].astype(BF16),
                                     gq, gk, gsum, gexp, kext, qext, bsz)
            bound = (HEAD_DIM * BOUND_MARGIN * jnp.max(jnp.abs(gq)) * jnp.max(jnp.abs(gk))).reshape(1)
            o = _attention(slopes, bound, q1, q2, k1.reshape(bsz, s, d), k2.reshape(bsz, s, d), v,
                           _row(attn_lq1[j]), _row(attn_lk1[j]), _row(attn_lq2[j]), _row(attn_lk2[j]),
                           attn_subln_g[j].reshape(-1, 1), lambda_init)
            x2 = ffn(x2, i, 1, proj=(o.reshape(bsz * s, d), attn_w_out[j].astype(BF16)))
    return x2.reshape(bsz, s, d)
```

```python
import functools
import math

import jax
import jax.numpy as jnp
import numpy as np
from jax import lax
from jax.experimental import pallas as pl
from jax.experimental.pallas import tpu as pltpu

EPS = 1e-6
CONV_WIDTH = 31
HEAD_DIM = 64
HEAD_WIDTH = 2 * HEAD_DIM
LANES = 128
MXU_COLS = 256
BF16_SUBLANES = 16
V_ROWS = HEAD_WIDTH + BF16_SUBLANES
N_POS_PIECES = 3
HALO = 32
NEG_BIG = -1e30
LOG2E = 1.4426950408889634
BOUND_MARGIN = 1.02
MAX_SAFE_BOUND = 40.0

TM = 512
FFN_TM = 1024
FFN_CHUNKS = 4
TK = 512
Q_SUB = 2
TQ = Q_SUB * TK
assert TM == TK
assert Q_SUB == 2
BOUNDED_GROUP = 4
CONV_ROWS = 64
VMEM_LIMIT = 56 * 1024 * 1024

F32 = jnp.float32
BF16 = jnp.bfloat16


def _rms(x, g):
    return x * lax.rsqrt(jnp.mean(x * x, axis=-1, keepdims=True) + EPS) * g


def _resident(shape, lead=()):
    index = tuple(lead) + (0,) * len(shape)
    return pl.BlockSpec((None,) * len(lead) + tuple(shape), lambda *_: index, pipeline_mode=pl.Buffered(1))


def _ffn_chunks(f, n_chunks):
    tiles = f // MXU_COLS
    assert tiles * MXU_COLS == f
    edges = [MXU_COLS * (tiles * c // n_chunks) for c in range(n_chunks + 1)]
    return list(zip(edges[:-1], edges[1:]))


def _ffn_kernel(*refs, n_chunks, project):
    if project:
        x_ref, a_ref, wp_ref, g_ref, win_ref, wo_ref, o_ref = refs
        x = x_ref[...] + jnp.dot(a_ref[...], wp_ref[...], preferred_element_type=F32)
    else:
        x_ref, g_ref, win_ref, wo_ref, o_ref = refs
        x = x_ref[...]
    f = wo_ref.shape[0]
    h = _rms(x, g_ref[...]).astype(BF16)
    y = jnp.zeros(x.shape, F32)
    for lo, hi in _ffn_chunks(f, n_chunks):
        gate = jnp.dot(h, win_ref[:, lo:hi], preferred_element_type=F32)
        up = jnp.dot(h, win_ref[:, f + lo:f + hi], preferred_element_type=F32)
        act = (gate * jax.nn.sigmoid(gate) * up).astype(BF16)
        y = y + jnp.dot(act, wo_ref[lo:hi, :], preferred_element_type=F32)
    o_ref[...] = x + 0.5 * y


def _ffn(x2, g, w_in, w_out, which, proj=None):
    t, d = x2.shape
    f = w_out.shape[-2]
    row = pl.BlockSpec((FFN_TM, d), lambda i: (i, 0))
    proj_specs = [] if proj is None else [row, _resident((d, d))]
    proj_args = () if proj is None else proj
    return pl.pallas_call(
        functools.partial(_ffn_kernel, n_chunks=FFN_CHUNKS, project=proj is not None),
        grid=(t // FFN_TM,),
        in_specs=[row, *proj_specs, _resident((1, d)),
                  _resident((d, 2 * f), which), _resident((f, d), which)],
        out_specs=row,
        out_shape=jax.ShapeDtypeStruct((t, d), F32),
        compiler_params=pltpu.CompilerParams(
            dimension_semantics=("arbitrary",), vmem_limit_bytes=VMEM_LIMIT),
        name="ffn" if proj is None else "proj_ffn",
    )(x2, *proj_args, g, w_in, w_out)


def _conv_kernel(x_ref, g_ref, win_ref, bin_ref, dw_ref, dwb_ref,
                 lng_ref, lnb_ref, wo_ref, bo_ref, o_ref, ubuf, cbuf):
    tm, d = x_ref.shape[1], x_ref.shape[2]
    n_lane_chunks = d // LANES

    @pl.when(pl.program_id(1) == 0)
    def _():
        ubuf[:, 0:HALO, :] = jnp.zeros((n_lane_chunks, HALO, LANES), F32)

    x = x_ref[0]
    h = _rms(x, g_ref[...]).astype(BF16)

    def project(cb):
        lo, hi = cb * MXU_COLS, (cb + 1) * MXU_COLS
        a = jnp.dot(h, win_ref[:, lo:hi], preferred_element_type=F32) + bin_ref[:, lo:hi]
        b = jnp.dot(h, win_ref[:, d + lo:d + hi], preferred_element_type=F32) + bin_ref[:, d + lo:d + hi]
        return a, b

    def conv_lane_chunk(c):
        for r0 in range(0, tm, CONV_ROWS):
            acc = jnp.zeros((CONV_ROWS, LANES), F32)
            for k in range(CONV_WIDTH):
                start = HALO + r0 - (CONV_WIDTH - 1) + k
                acc = acc + dw_ref[c, k:k + 1, :] * ubuf[c, start:start + CONV_ROWS, :]
            cbuf[c, r0:r0 + CONV_ROWS, :] = acc
        ubuf[c, 0:HALO, :] = ubuf[c, tm:tm + HALO, :]

    chunks_per_block = MXU_COLS // LANES
    n_blocks = d // MXU_COLS
    ab = project(0)
    for cb in range(n_blocks):
        a, b = ab
        u = a * jax.nn.sigmoid(b)
        for s in range(chunks_per_block):
            ubuf[cb * chunks_per_block + s, HALO:HALO + tm, :] = u[:, s * LANES:(s + 1) * LANES]
        if cb + 1 < n_blocks:
            ab = project(cb + 1)
        for s in range(chunks_per_block):
            conv_lane_chunk(cb * chunks_per_block + s)

    y = jnp.concatenate([cbuf[c] for c in range(n_lane_chunks)], axis=1) + dwb_ref[...]
    mu = jnp.mean(y, axis=-1, keepdims=True)
    yc = y - mu
    var = jnp.mean(yc * yc, axis=-1, keepdims=True)
    z = yc * lax.rsqrt(var + EPS) * lng_ref[...] + lnb_ref[...]
    z = (z * jax.nn.sigmoid(z)).astype(BF16)
    m = jnp.dot(z, wo_ref[...], preferred_element_type=F32) + bo_ref[...]
    o_ref[0] = x + m


def _conv_mixer(x3, g, w_in, b_in, dw3, dwb, lng, lnb, wo, bo):
    bsz, s, d = x3.shape
    nl = d // LANES
    kpad = dw3.shape[1]
    return pl.pallas_call(
        _conv_kernel,
        grid=(bsz, s // TM),
        in_specs=[
            pl.BlockSpec((1, TM, d), lambda b, i: (b, i, 0)),
            _resident((1, d)),
            _resident((d, 2 * d)), _resident((1, 2 * d)),
            _resident((nl, kpad, LANES)),
            _resident((1, d)), _resident((1, d)), _resident((1, d)),
            _resident((d, d)), _resident((1, d)),
        ],
        out_specs=pl.BlockSpec((1, TM, d), lambda b, i: (b, i, 0)),
        out_shape=jax.ShapeDtypeStruct((bsz, s, d), F32),
        scratch_shapes=[
            pltpu.VMEM((nl, HALO + TM, LANES), F32),
            pltpu.VMEM((nl, TM, LANES), F32),
        ],
        compiler_params=pltpu.CompilerParams(
            dimension_semantics=("arbitrary", "arbitrary"), vmem_limit_bytes=VMEM_LIMIT),
        name="conv_mixer",
    )(x3, g, w_in, b_in, dw3, dwb, lng, lnb, wo, bo)


def _split_dot(v, w_ref):
    hi = v.astype(BF16)
    lo = (v - hi.astype(F32)).astype(BF16)
    return (jnp.dot(hi, w_ref[...], preferred_element_type=F32)
            + jnp.dot(lo, w_ref[...], preferred_element_type=F32))


def _group_rsqrt(v, gsum_ref, gexp_ref):
    ms = _split_dot(v * v, gsum_ref) * (1.0 / HEAD_DIM)
    return _split_dot(lax.rsqrt(ms + EPS), gexp_ref)


def _qkv_kernel(x_ref, g_ref, w_ref, gqk_ref, gsum_ref, gexp_ref, kext_ref, qext_ref,
                q1_ref, q2_ref, k1_ref, k2_ref, v_ref):
    n_heads, tm = q1_ref.shape[1], q1_ref.shape[4]
    d = x_ref.shape[1]
    x = x_ref[...]
    h = _rms(x, g_ref[...]).astype(BF16)
    q = jnp.dot(h, w_ref[:, 0:d], preferred_element_type=F32)
    k = jnp.dot(h, w_ref[:, d:2 * d], preferred_element_type=F32)
    v = jnp.dot(h, w_ref[:, 2 * d:3 * d], preferred_element_type=F32)
    kn = k * _group_rsqrt(k, gsum_ref, gexp_ref) * gqk_ref[...]
    lane_first = (lax.broadcasted_iota(jnp.int32, kn.shape, 1) % HEAD_WIDTH) < HEAD_DIM
    kext = kext_ref[...]
    k1_ref[...] = jnp.where(lane_first, kn, kext).astype(BF16)
    k2_ref[...] = jnp.where(lane_first, kext, kn).astype(BF16)
    qt = q.T.reshape(d // HEAD_DIM, HEAD_DIM, tm)
    qt = qt * lax.rsqrt(jnp.mean(qt * qt, axis=1, keepdims=True) + EPS)
    qt = qt.reshape(n_heads, HEAD_WIDTH, tm)
    row_first = lax.broadcasted_iota(jnp.int32, qt.shape, 1) < HEAD_DIM
    qext = jnp.broadcast_to(qext_ref[...].reshape(n_heads, HEAD_WIDTH, 1), qt.shape)
    q1_ref[0, :, 0] = jnp.where(row_first, qt, qext).astype(BF16)
    q2_ref[0, :, 0] = jnp.where(row_first, qext, qt).astype(BF16)
    v_ref[0, :, 0, 0:HEAD_WIDTH, :] = v.T.reshape(n_heads, HEAD_WIDTH, tm).astype(BF16)
    ones_row = lax.broadcasted_iota(jnp.int32, (n_heads, V_ROWS - HEAD_WIDTH, tm), 1) == 0
    v_ref[0, :, 0, HEAD_WIDTH:V_ROWS, :] = jnp.where(ones_row, 1.0, 0.0).astype(BF16)


def _qkv(x2, g, w_qkv, gqk, gsum, gexp, kext, qext, bsz):
    t, d = x2.shape
    n_heads = d // HEAD_WIDTH
    n_tiles = t // bsz // TM
    row = pl.BlockSpec((TM, d), lambda i: (i, 0))
    slab = lambda rows: pl.BlockSpec((1, n_heads, 1, rows, TM), lambda i: (i // n_tiles, 0, i % n_tiles, 0, 0))
    slab_shape = lambda rows: jax.ShapeDtypeStruct((bsz, n_heads, n_tiles, rows, TM), BF16)
    rows_shape = jax.ShapeDtypeStruct((t, d), BF16)
    return pl.pallas_call(
        _qkv_kernel,
        grid=(t // TM,),
        in_specs=[row, _resident((1, d)), _resident((d, 3 * d)),
                  _resident((1, d)), _resident((d, LANES)), _resident((LANES, d)),
                  _resident((TM, d)), _resident((d, 1))],
        out_specs=[slab(HEAD_WIDTH), slab(HEAD_WIDTH), row, row, slab(V_ROWS)],
        out_shape=[slab_shape(HEAD_WIDTH), slab_shape(HEAD_WIDTH), rows_shape, rows_shape, slab_shape(V_ROWS)],
        compiler_params=pltpu.CompilerParams(
            dimension_semantics=("arbitrary",), vmem_limit_bytes=VMEM_LIMIT),
        name="qkv_proj",
    )(x2, g, w_qkv, gqk, gsum, gexp, kext, qext)


def _attn_kernel(slope_ref, bound_ref, q1_ref, q2_ref, k1_ref, k2_ref, v_ref, lq1_ref, lk1_ref, lq2_ref,
                 lk2_ref, subg_ref, o_ref, z_sc, p_sc, m_sc, a_sc, acc_sc, *, lambda_init):
    hd = pl.program_id(1)
    qi = pl.program_id(2)
    slope = slope_ref[hd] * LOG2E
    tile_step = slope * TK
    first_tile = qi * Q_SUB

    qs = tuple(jnp.concatenate([q_ref[0, 0, s] for s in range(Q_SUB)], axis=1)
               for q_ref in (q1_ref, q2_ref))
    ks = (k1_ref, k2_ref)

    def scores(j, c):
        kt = ks[c][0, pl.ds(pl.multiple_of(j * TK, TK), TK), :]
        return jnp.dot(kt, qs[c], preferred_element_type=F32)

    def causal(z, key_offset):
        krow = lax.broadcasted_iota(jnp.int32, (TK, TQ), 0)
        qcol = lax.broadcasted_iota(jnp.int32, (TK, TQ), 1)
        return jnp.where(krow + key_offset <= qcol, z, NEG_BIG)

    acc_sc[...] = jnp.zeros(acc_sc.shape, F32)

    def bounded_tiles(j0, n_tiles, with_diagonal):
        qoff = lax.broadcasted_iota(jnp.int32, (1, TQ), 1).astype(F32)
        ref = bound_ref[0] + slope * qoff
        n_wide = n_tiles - 1 if with_diagonal else n_tiles
        for c in range(2):
            ps = []
            for t in range(n_wide):
                z = scores(j0 + t, c)
                if with_diagonal and t == n_wide - 1:
                    z = causal(z, 0)
                off = tile_step * (j0 + t - first_tile).astype(F32) - ref
                ps.append(jnp.exp2(z + off).astype(BF16))
            vt = jnp.concatenate([v_ref[0, 0, j0 + t] for t in range(n_wide)], axis=1)
            acc_sc[c] += jnp.dot(vt, jnp.concatenate(ps, axis=0), preferred_element_type=F32)
            if with_diagonal:
                jl = j0 + n_wide
                kt = ks[c][0, pl.ds(pl.multiple_of(jl * TK, TK), TK), :]
                z = jnp.dot(kt, qs[c][:, TK:], preferred_element_type=F32)
                krow = lax.broadcasted_iota(jnp.int32, (TK, TK), 0)
                qcol = lax.broadcasted_iota(jnp.int32, (TK, TK), 1)
                z = jnp.where(krow <= qcol, z, NEG_BIG)
                off = tile_step * (jl - first_tile).astype(F32) - ref[:, TK:]
                p = jnp.exp2(z + off).astype(BF16)
                acc_sc[c, :, TK:] += jnp.dot(v_ref[0, 0, jl], p, preferred_element_type=F32)

    def bounded_path():
        def group(t, carry):
            bounded_tiles(BOUNDED_GROUP * t, BOUNDED_GROUP, with_diagonal=False)
            return carry

        lax.fori_loop(0, first_tile // BOUNDED_GROUP, group, 0)
        for rest in range(0, BOUNDED_GROUP, Q_SUB):
            pl.when(first_tile % BOUNDED_GROUP == rest)(
                functools.partial(bounded_tiles, first_tile - rest, rest + Q_SUB, with_diagonal=True))

    def softmax(z, j, c, key_offset):
        if key_offset is not None:
            z = causal(z, key_offset)
        shift = tile_step * (j - first_tile).astype(F32)
        m_old = m_sc[c]
        m_new = jnp.maximum(m_old, jnp.max(z, axis=0, keepdims=True) + shift)
        m_sc[c] = m_new
        return jnp.exp2(z - (m_new - shift)).astype(BF16), jnp.exp2(m_old - m_new)

    def accumulate(c, j, p, alpha):
        acc_sc[c] = alpha * acc_sc[c] + jnp.dot(v_ref[0, 0, j], p, preferred_element_type=F32)

    def step(j, par, key_offset=None, last=False):
        zb = scores(j, 1)
        pa, alpha_a = softmax(z_sc[par], j, 0, key_offset)
        accumulate(1, jnp.maximum(j - 1, 0), p_sc[1 - par], a_sc[...])
        if not last:
            z_sc[1 - par] = scores(j + 1, 0)
        pb, alpha_b = softmax(zb, j, 1, key_offset)
        accumulate(0, j, pa, alpha_a)
        if last:
            accumulate(1, j, pb, alpha_b)
        else:
            p_sc[par] = pb
            a_sc[...] = alpha_b

    def running_max_path():
        m_sc[...] = jnp.full(m_sc.shape, NEG_BIG, F32)
        a_sc[...] = jnp.ones(a_sc.shape, F32)
        p_sc[1] = jnp.zeros(p_sc.shape[1:], BF16)
        z_sc[0] = scores(0, 0)

        def pair(t, carry):
            step(2 * t, 0)
            step(2 * t + 1, 1)
            return carry

        lax.fori_loop(0, first_tile // 2, pair, 0)
        for s in range(Q_SUB):
            step(first_tile + s, s % 2, key_offset=s * TK, last=s == Q_SUB - 1)

    bounded = bound_ref[0] <= MAX_SAFE_BOUND
    pl.when(bounded)(bounded_path)
    pl.when(jnp.logical_not(bounded))(running_max_path)

    lam = (jnp.exp(jnp.sum(lq1_ref[...] * lk1_ref[...], keepdims=True))
           - jnp.exp(jnp.sum(lq2_ref[...] * lk2_ref[...], keepdims=True))
           + lambda_init)
    acc_a, acc_b = acc_sc[0], acc_sc[1]
    ot = (acc_a[:HEAD_WIDTH] / acc_a[HEAD_WIDTH:HEAD_WIDTH + 1]
          - lam * (acc_b[:HEAD_WIDTH] / acc_b[HEAD_WIDTH:HEAD_WIDTH + 1]))
    ot = ot * lax.rsqrt(jnp.mean(ot * ot, axis=0, keepdims=True) + EPS) * subg_ref[...]
    o_ref[0] = (ot * (1.0 - lambda_init)).T.astype(BF16)


def _attention(slopes, bound, q1, q2, k1, k2, v, lq1, lk1, lq2, lk2, subg, lambda_init):
    bsz, n_heads, n_tiles = q1.shape[:3]
    s, d = k1.shape[1], k1.shape[2]
    qspec = pl.BlockSpec((1, 1, Q_SUB, HEAD_WIDTH, TK), lambda b, h, i: (b, h, i, 0, 0))
    kspec = pl.BlockSpec((1, s, HEAD_WIDTH), lambda b, h, i: (b, 0, h))
    vspec = pl.BlockSpec((1, 1, n_tiles, V_ROWS, TK), lambda b, h, i: (b, h, 0, 0, 0))
    small = lambda n: pl.BlockSpec((1, n), lambda b, h, i: (0, 0))
    return pl.pallas_call(
        functools.partial(_attn_kernel, lambda_init=lambda_init),
        grid=(bsz, n_heads, s // TQ),
        in_specs=[
            pl.BlockSpec(memory_space=pltpu.SMEM), pl.BlockSpec(memory_space=pltpu.SMEM),
            qspec, qspec, kspec, kspec, vspec,
            small(HEAD_DIM), small(HEAD_DIM), small(HEAD_DIM), small(HEAD_DIM),
            pl.BlockSpec((HEAD_WIDTH, 1), lambda b, h, i: (0, 0)),
        ],
        out_specs=pl.BlockSpec((1, TQ, HEAD_WIDTH), lambda b, h, i: (b, i, h)),
        out_shape=jax.ShapeDtypeStruct((bsz, s, d), BF16),
        scratch_shapes=[
            pltpu.VMEM((2, TK, TQ), F32),
            pltpu.VMEM((2, TK, TQ), BF16),
            pltpu.VMEM((2, 1, TQ), F32),
            pltpu.VMEM((1, TQ), F32),
            pltpu.VMEM((2, V_ROWS, TQ), F32),
        ],
        compiler_params=pltpu.CompilerParams(
            dimension_semantics=("arbitrary", "arbitrary", "arbitrary"),
            vmem_limit_bytes=VMEM_LIMIT),
        name="diff_attn",
    )(slopes, bound, q1, q2, k1, k2, v, lq1, lk1, lq2, lk2, subg)


def _row(v):
    return v.reshape(1, -1)


def _alibi_tables(n_heads):
    slopes = np.exp2(-8.0 * np.arange(1, n_heads + 1) / n_heads)
    assert np.all(np.log2(slopes) == np.round(np.log2(slopes))), "slopes must be exact in bf16"
    pos = np.arange(TM, dtype=np.float32) * np.float32(LOG2E)
    pieces = []
    for _ in range(N_POS_PIECES):
        piece = pos.astype(BF16).astype(np.float32)
        pieces.append(piece)
        pos = pos - piece
    cols = np.stack(pieces, axis=1)
    half = np.pad(cols, ((0, 0), (0, HEAD_DIM - N_POS_PIECES)))
    kext = np.tile(np.concatenate([half, half], axis=1), (1, n_heads))
    slot = (np.arange(HEAD_WIDTH) % HEAD_DIM) < N_POS_PIECES
    qext = (slopes[:, None] * slot[None, :]).reshape(-1, 1)
    return jnp.asarray(slopes, F32), jnp.asarray(kext, F32), jnp.asarray(qext, F32)


def _group_tables(d):
    group = np.arange(d) // HEAD_DIM
    gsum = (group[:, None] == np.arange(LANES)[None, :]).astype(np.float32)
    return jnp.asarray(gsum, BF16), jnp.asarray(gsum.T, BF16)


def kernel(x, ffn_norm, ffn_w_in, ffn_w_out, mix_norm, conv_w_in, conv_b_in, conv_dw, conv_dw_b, conv_ln_g, conv_ln_b, conv_w_out, conv_b_out, attn_w_qkv, attn_q_norm, attn_k_norm, attn_lq1, attn_lk1, attn_lq2, attn_lk2, attn_subln_g, attn_w_out):
    bsz, s, d = x.shape
    depth = ffn_norm.shape[0]
    n_heads = d // HEAD_WIDTH
    n_mixers = 2
    ffn_w_in = ffn_w_in.astype(BF16)
    ffn_w_out = ffn_w_out.astype(BF16)

    def ffn(x2, i, j, proj=None):
        return _ffn(x2, _row(ffn_norm[i, j]), ffn_w_in, ffn_w_out, (i, j), proj)

    x2 = x.reshape(bsz * s, d)
    for i in range(depth):
        x2 = ffn(x2, i, 0)
        j = i // n_mixers
        if i % n_mixers == 0:
            kpad = -(-CONV_WIDTH // 8) * 8
            dw3 = jnp.pad(conv_dw[j], ((0, kpad - CONV_WIDTH), (0, 0)))
            dw3 = dw3.reshape(kpad, d // LANES, LANES).transpose(1, 0, 2)
            x2 = _conv_mixer(
                x2.reshape(bsz, s, d), _row(mix_norm[i]), conv_w_in[j].astype(BF16), _row(conv_b_in[j]),
                dw3, _row(conv_dw_b[j]), _row(conv_ln_g[j]), _row(conv_ln_b[j]),
                conv_w_out[j].astype(BF16), _row(conv_b_out[j])).reshape(bsz * s, d)
            x2 = ffn(x2, i, 1)
        else:
            lambda_init = 0.8 - 0.6 * math.exp(-0.3 * i)
            gqk = _row(jnp.tile(attn_q_norm[j] * attn_k_norm[j], d // HEAD_DIM)) * (LOG2E / math.sqrt(HEAD_DIM))
            gsum, gexp = _group_tables(d)
            slopes, kext, qext = _alibi_tables(n_heads)
            q1, q2, k1, k2, v = _qkv(x2, _row(mix_norm[i]), attn_w_qkv[j].astype(BF16),
                                     gqk, gsum, gexp, kext, qext, bsz)
            bound = (HEAD_DIM * BOUND_MARGIN * jnp.max(jnp.abs(gqk))).reshape(1)
            o = _attention(slopes, bound, q1, q2, k1.reshape(bsz, s, d), k2.reshape(bsz, s, d), v,
                           _row(attn_lq1[j]), _row(attn_lk1[j]), _row(attn_lq2[j]), _row(attn_lk2[j]),
                           attn_subln_g[j].reshape(-1, 1), lambda_init)
            x2 = ffn(x2, i, 1, proj=(o.reshape(bsz * s, d), attn_w_out[j].astype(BF16)))
    return x2.reshape(bsz, s, d)
```

```python
import functools
import math

import jax
import jax.numpy as jnp
import numpy as np
from jax import lax
from jax.experimental import pallas as pl
from jax.experimental.pallas import tpu as pltpu

EPS = 1e-6
CONV_WIDTH = 31
HEAD_DIM = 64
HEAD_WIDTH = 2 * HEAD_DIM
LANES = 128
MXU_COLS = 256
BF16_SUBLANES = 16
V_ROWS = HEAD_WIDTH + BF16_SUBLANES
N_POS_PIECES = 3
HALO = 32
NEG_BIG = -1e30
LOG2E = 1.4426950408889634
BOUND_MARGIN = 1.02
MAX_SAFE_BOUND = 40.0

TM = 512
FFN_TM = 1024
FFN_CHUNKS = 4
TK = 512
Q_SUB = 2
TQ = Q_SUB * TK
assert TM == TK
assert Q_SUB == 2
BOUNDED_GROUP = 4
CONV_ROWS = 64
VMEM_LIMIT = 56 * 1024 * 1024

F32 = jnp.float32
BF16 = jnp.bfloat16


def _rms(x, g):
    return x * lax.rsqrt(jnp.mean(x * x, axis=-1, keepdims=True) + EPS) * g


def _resident(shape, lead=()):
    index = tuple(lead) + (0,) * len(shape)
    return pl.BlockSpec((None,) * len(lead) + tuple(shape), lambda *_: index, pipeline_mode=pl.Buffered(1))


def _ffn_chunks(f, n_chunks):
    tiles = f // MXU_COLS
    assert tiles * MXU_COLS == f
    edges = [MXU_COLS * (tiles * c // n_chunks) for c in range(n_chunks + 1)]
    return list(zip(edges[:-1], edges[1:]))


def _ffn_kernel(*refs, n_chunks, project):
    if project:
        x_ref, a_ref, wp_ref, g_ref, win_ref, wo_ref, o_ref = refs
        x = x_ref[...] + jnp.dot(a_ref[...], wp_ref[...], preferred_element_type=F32)
    else:
        x_ref, g_ref, win_ref, wo_ref, o_ref = refs
        x = x_ref[...]
    f = wo_ref.shape[0]
    h = _rms(x, g_ref[...]).astype(BF16)
    y = jnp.zeros(x.shape, F32)
    for lo, hi in _ffn_chunks(f, n_chunks):
        gate = jnp.dot(h, win_ref[:, lo:hi], preferred_element_type=F32)
        up = jnp.dot(h, win_ref[:, f + lo:f + hi], preferred_element_type=F32)
        act = (gate * jax.nn.sigmoid(gate) * up).astype(BF16)
        y = y + jnp.dot(act, wo_ref[lo:hi, :], preferred_element_type=F32)
    o_ref[...] = x + 0.5 * y


def _ffn(x2, g, w_in, w_out, which, proj=None):
    t, d = x2.shape
    f = w_out.shape[-2]
    row = pl.BlockSpec((FFN_TM, d), lambda i: (i, 0))
    proj_specs = [] if proj is None else [row, _resident((d, d))]
    proj_args = () if proj is None else proj
    return pl.pallas_call(
        functools.partial(_ffn_kernel, n_chunks=FFN_CHUNKS, project=proj is not None),
        grid=(t // FFN_TM,),
        in_specs=[row, *proj_specs, _resident((1, d)),
                  _resident((d, 2 * f), which), _resident((f, d), which)],
        out_specs=row,
        out_shape=jax.ShapeDtypeStruct((t, d), F32),
        compiler_params=pltpu.CompilerParams(
            dimension_semantics=("arbitrary",), vmem_limit_bytes=VMEM_LIMIT),
        name="ffn" if proj is None else "proj_ffn",
    )(x2, *proj_args, g, w_in, w_out)


def _conv_kernel(x_ref, g_ref, win_ref, bin_ref, dw_ref, dwb_ref,
                 lng_ref, lnb_ref, wo_ref, bo_ref, o_ref, ubuf, cbuf):
    tm, d = x_ref.shape[1], x_ref.shape[2]
    n_lane_chunks = d // LANES

    @pl.when(pl.program_id(1) == 0)
    def _():
        ubuf[:, 0:HALO, :] = jnp.zeros((n_lane_chunks, HALO, LANES), F32)

    x = x_ref[0]
    h = _rms(x, g_ref[...]).astype(BF16)

    def project(cb):
        lo, hi = cb * MXU_COLS, (cb + 1) * MXU_COLS
        a = jnp.dot(h, win_ref[:, lo:hi], preferred_element_type=F32) + bin_ref[:, lo:hi]
        b = jnp.dot(h, win_ref[:, d + lo:d + hi], preferred_element_type=F32) + bin_ref[:, d + lo:d + hi]
        return a, b

    def conv_lane_chunk(c):
        for r0 in range(0, tm, CONV_ROWS):
            acc = jnp.zeros((CONV_ROWS, LANES), F32)
            for k in range(CONV_WIDTH):
                start = HALO + r0 - (CONV_WIDTH - 1) + k
                acc = acc + dw_ref[c, k:k + 1, :] * ubuf[c, start:start + CONV_ROWS, :]
            cbuf[c, r0:r0 + CONV_ROWS, :] = acc
        ubuf[c, 0:HALO, :] = ubuf[c, tm:tm + HALO, :]

    chunks_per_block = MXU_COLS // LANES
    n_blocks = d // MXU_COLS
    ab = project(0)
    for cb in range(n_blocks):
        a, b = ab
        u = a * jax.nn.sigmoid(b)
        for s in range(chunks_per_block):
            ubuf[cb * chunks_per_block + s, HALO:HALO + tm, :] = u[:, s * LANES:(s + 1) * LANES]
        if cb + 1 < n_blocks:
            ab = project(cb + 1)
        for s in range(chunks_per_block):
            conv_lane_chunk(cb * chunks_per_block + s)

    y = jnp.concatenate([cbuf[c] for c in range(n_lane_chunks)], axis=1) + dwb_ref[...]
    mu = jnp.mean(y, axis=-1, keepdims=True)
    yc = y - mu
    var = jnp.mean(yc * yc, axis=-1, keepdims=True)
    z = yc * lax.rsqrt(var + EPS) * lng_ref[...] + lnb_ref[...]
    z = (z * jax.nn.sigmoid(z)).astype(BF16)
    m = jnp.dot(z, wo_ref[...], preferred_element_type=F32) + bo_ref[...]
    o_ref[0] = x + m


def _conv_mixer(x3, g, w_in, b_in, dw3, dwb, lng, lnb, wo, bo):
    bsz, s, d = x3.shape
    nl = d // LANES
    kpad = dw3.shape[1]
    return pl.pallas_call(
        _conv_kernel,
        grid=(bsz, s // TM),
        in_specs=[
            pl.BlockSpec((1, TM, d), lambda b, i: (b, i, 0)),
            _resident((1, d)),
            _resident((d, 2 * d)), _resident((1, 2 * d)),
            _resident((nl, kpad, LANES)),
            _resident((1, d)), _resident((1, d)), _resident((1, d)),
            _resident((d, d)), _resident((1, d)),
        ],
        out_specs=pl.BlockSpec((1, TM, d), lambda b, i: (b, i, 0)),
        out_shape=jax.ShapeDtypeStruct((bsz, s, d), F32),
        scratch_shapes=[
            pltpu.VMEM((nl, HALO + TM, LANES), F32),
            pltpu.VMEM((nl, TM, LANES), F32),
        ],
        compiler_params=pltpu.CompilerParams(
            dimension_semantics=("arbitrary", "arbitrary"), vmem_limit_bytes=VMEM_LIMIT),
        name="conv_mixer",
    )(x3, g, w_in, b_in, dw3, dwb, lng, lnb, wo, bo)


def _qkv_kernel(x_ref, g_ref, w_ref, gqk_ref, kext_ref, qext_ref,
                q1_ref, q2_ref, k1_ref, k2_ref, v_ref):
    n_heads, tm = q1_ref.shape[1], q1_ref.shape[4]
    d = x_ref.shape[1]
    x = x_ref[...]
    h = _rms(x, g_ref[...]).astype(BF16)
    q = jnp.dot(h, w_ref[:, 0:d], preferred_element_type=F32)
    k = jnp.dot(h, w_ref[:, d:2 * d], preferred_element_type=F32)
    v = jnp.dot(h, w_ref[:, 2 * d:3 * d], preferred_element_type=F32)

    def qk_norm_t(a):
        at = a.T.reshape(d // HEAD_DIM, HEAD_DIM, tm)
        at = at * lax.rsqrt(jnp.mean(at * at, axis=1, keepdims=True) + EPS)
        return at.reshape(d, tm)

    qt = qk_norm_t(q).reshape(n_heads, HEAD_WIDTH, tm)
    row_first = lax.broadcasted_iota(jnp.int32, qt.shape, 1) < HEAD_DIM
    qext = jnp.broadcast_to(qext_ref[...].reshape(n_heads, HEAD_WIDTH, 1), qt.shape)
    q1_ref[0, :, 0] = jnp.where(row_first, qt, qext).astype(BF16)
    q2_ref[0, :, 0] = jnp.where(row_first, qext, qt).astype(BF16)
    kn = qk_norm_t(k).T * gqk_ref[...]
    lane_first = (lax.broadcasted_iota(jnp.int32, kn.shape, 1) % HEAD_WIDTH) < HEAD_DIM
    kext = kext_ref[...]
    k1_ref[...] = jnp.where(lane_first, kn, kext).astype(BF16)
    k2_ref[...] = jnp.where(lane_first, kext, kn).astype(BF16)
    v_ref[0, :, 0, 0:HEAD_WIDTH, :] = v.T.reshape(n_heads, HEAD_WIDTH, tm).astype(BF16)
    ones_row = lax.broadcasted_iota(jnp.int32, (n_heads, V_ROWS - HEAD_WIDTH, tm), 1) == 0
    v_ref[0, :, 0, HEAD_WIDTH:V_ROWS, :] = jnp.where(ones_row, 1.0, 0.0).astype(BF16)


def _qkv(x2, g, w_qkv, gqk, kext, qext, bsz):
    t, d = x2.shape
    n_heads = d // HEAD_WIDTH
    n_tiles = t // bsz // TM
    row = pl.BlockSpec((TM, d), lambda i: (i, 0))
    slab = lambda rows: pl.BlockSpec((1, n_heads, 1, rows, TM), lambda i: (i // n_tiles, 0, i % n_tiles, 0, 0))
    slab_shape = lambda rows: jax.ShapeDtypeStruct((bsz, n_heads, n_tiles, rows, TM), BF16)
    rows_shape = jax.ShapeDtypeStruct((t, d), BF16)
    return pl.pallas_call(
        _qkv_kernel,
        grid=(t // TM,),
        in_specs=[row, _resident((1, d)), _resident((d, 3 * d)),
                  _resident((1, d)), _resident((TM, d)), _resident((d, 1))],
        out_specs=[slab(HEAD_WIDTH), slab(HEAD_WIDTH), row, row, slab(V_ROWS)],
        out_shape=[slab_shape(HEAD_WIDTH), slab_shape(HEAD_WIDTH), rows_shape, rows_shape, slab_shape(V_ROWS)],
        compiler_params=pltpu.CompilerParams(
            dimension_semantics=("arbitrary",), vmem_limit_bytes=VMEM_LIMIT),
        name="qkv_proj",
    )(x2, g, w_qkv, gqk, kext, qext)


def _attn_kernel(slope_ref, bound_ref, q1_ref, q2_ref, k1_ref, k2_ref, v_ref, lq1_ref, lk1_ref, lq2_ref,
                 lk2_ref, subg_ref, o_ref, z_sc, p_sc, m_sc, a_sc, acc_sc, *, lambda_init):
    hd = pl.program_id(1)
    qi = pl.program_id(2)
    slope = slope_ref[hd] * LOG2E
    tile_step = slope * TK
    first_tile = qi * Q_SUB

    qs = tuple(jnp.concatenate([q_ref[0, 0, s] for s in range(Q_SUB)], axis=1)
               for q_ref in (q1_ref, q2_ref))
    ks = (k1_ref, k2_ref)

    def scores(j, c):
        kt = ks[c][0, pl.ds(pl.multiple_of(j * TK, TK), TK), :]
        return jnp.dot(kt, qs[c], preferred_element_type=F32)

    def causal(z, key_offset):
        krow = lax.broadcasted_iota(jnp.int32, (TK, TQ), 0)
        qcol = lax.broadcasted_iota(jnp.int32, (TK, TQ), 1)
        return jnp.where(krow + key_offset <= qcol, z, NEG_BIG)

    acc_sc[...] = jnp.zeros(acc_sc.shape, F32)

    def bounded_tiles(j0, n_tiles, with_diagonal):
        qoff = lax.broadcasted_iota(jnp.int32, (1, TQ), 1).astype(F32)
        ref = bound_ref[0] + slope * qoff
        n_wide = n_tiles - 1 if with_diagonal else n_tiles
        for c in range(2):
            ps = []
            for t in range(n_wide):
                z = scores(j0 + t, c)
                if with_diagonal and t == n_wide - 1:
                    z = causal(z, 0)
                off = tile_step * (j0 + t - first_tile).astype(F32) - ref
                ps.append(jnp.exp2(z + off).astype(BF16))
            vt = jnp.concatenate([v_ref[0, 0, j0 + t] for t in range(n_wide)], axis=1)
            acc_sc[c] += jnp.dot(vt, jnp.concatenate(ps, axis=0), preferred_element_type=F32)
            if with_diagonal:
                jl = j0 + n_wide
                kt = ks[c][0, pl.ds(pl.multiple_of(jl * TK, TK), TK), :]
                z = jnp.dot(kt, qs[c][:, TK:], preferred_element_type=F32)
                krow = lax.broadcasted_iota(jnp.int32, (TK, TK), 0)
                qcol = lax.broadcasted_iota(jnp.int32, (TK, TK), 1)
                z = jnp.where(krow <= qcol, z, NEG_BIG)
                off = tile_step * (jl - first_tile).astype(F32) - ref[:, TK:]
                p = jnp.exp2(z + off).astype(BF16)
                acc_sc[c, :, TK:] += jnp.dot(v_ref[0, 0, jl], p, preferred_element_type=F32)

    def bounded_path():
        def group(t, carry):
            bounded_tiles(BOUNDED_GROUP * t, BOUNDED_GROUP, with_diagonal=False)
            return carry

        lax.fori_loop(0, first_tile // BOUNDED_GROUP, group, 0)
        for rest in range(0, BOUNDED_GROUP, Q_SUB):
            pl.when(first_tile % BOUNDED_GROUP == rest)(
                functools.partial(bounded_tiles, first_tile - rest, rest + Q_SUB, with_diagonal=True))

    def softmax(z, j, c, key_offset):
        if key_offset is not None:
            z = causal(z, key_offset)
        shift = tile_step * (j - first_tile).astype(F32)
        m_old = m_sc[c]
        m_new = jnp.maximum(m_old, jnp.max(z, axis=0, keepdims=True) + shift)
        m_sc[c] = m_new
        return jnp.exp2(z - (m_new - shift)).astype(BF16), jnp.exp2(m_old - m_new)

    def accumulate(c, j, p, alpha):
        acc_sc[c] = alpha * acc_sc[c] + jnp.dot(v_ref[0, 0, j], p, preferred_element_type=F32)

    def step(j, par, key_offset=None, last=False):
        zb = scores(j, 1)
        pa, alpha_a = softmax(z_sc[par], j, 0, key_offset)
        accumulate(1, jnp.maximum(j - 1, 0), p_sc[1 - par], a_sc[...])
        if not last:
            z_sc[1 - par] = scores(j + 1, 0)
        pb, alpha_b = softmax(zb, j, 1, key_offset)
        accumulate(0, j, pa, alpha_a)
        if last:
            accumulate(1, j, pb, alpha_b)
        else:
            p_sc[par] = pb
            a_sc[...] = alpha_b

    def running_max_path():
        m_sc[...] = jnp.full(m_sc.shape, NEG_BIG, F32)
        a_sc[...] = jnp.ones(a_sc.shape, F32)
        p_sc[1] = jnp.zeros(p_sc.shape[1:], BF16)
        z_sc[0] = scores(0, 0)

        def pair(t, carry):
            step(2 * t, 0)
            step(2 * t + 1, 1)
            return carry

        lax.fori_loop(0, first_tile // 2, pair, 0)
        for s in range(Q_SUB):
            step(first_tile + s, s % 2, key_offset=s * TK, last=s == Q_SUB - 1)

    bounded = bound_ref[0] <= MAX_SAFE_BOUND
    pl.when(bounded)(bounded_path)
    pl.when(jnp.logical_not(bounded))(running_max_path)

    lam = (jnp.exp(jnp.sum(lq1_ref[...] * lk1_ref[...], keepdims=True))
           - jnp.exp(jnp.sum(lq2_ref[...] * lk2_ref[...], keepdims=True))
           + lambda_init)
    acc_a, acc_b = acc_sc[0], acc_sc[1]
    ot = (acc_a[:HEAD_WIDTH] / acc_a[HEAD_WIDTH:HEAD_WIDTH + 1]
          - lam * (acc_b[:HEAD_WIDTH] / acc_b[HEAD_WIDTH:HEAD_WIDTH + 1]))
    ot = ot * lax.rsqrt(jnp.mean(ot * ot, axis=0, keepdims=True) + EPS) * subg_ref[...]
    o_ref[0] = (ot * (1.0 - lambda_init)).T.astype(BF16)


def _attention(slopes, bound, q1, q2, k1, k2, v, lq1, lk1, lq2, lk2, subg, lambda_init):
    bsz, n_heads, n_tiles = q1.shape[:3]
    s, d = k1.shape[1], k1.shape[2]
    qspec = pl.BlockSpec((1, 1, Q_SUB, HEAD_WIDTH, TK), lambda b, h, i: (b, h, i, 0, 0))
    kspec = pl.BlockSpec((1, s, HEAD_WIDTH), lambda b, h, i: (b, 0, h))
    vspec = pl.BlockSpec((1, 1, n_tiles, V_ROWS, TK), lambda b, h, i: (b, h, 0, 0, 0))
    small = lambda n: pl.BlockSpec((1, n), lambda b, h, i: (0, 0))
    return pl.pallas_call(
        functools.partial(_attn_kernel, lambda_init=lambda_init),
        grid=(bsz, n_heads, s // TQ),
        in_specs=[
            pl.BlockSpec(memory_space=pltpu.SMEM), pl.BlockSpec(memory_space=pltpu.SMEM),
            qspec, qspec, kspec, kspec, vspec,
            small(HEAD_DIM), small(HEAD_DIM), small(HEAD_DIM), small(HEAD_DIM),
            pl.BlockSpec((HEAD_WIDTH, 1), lambda b, h, i: (0, 0)),
        ],
        out_specs=pl.BlockSpec((1, TQ, HEAD_WIDTH), lambda b, h, i: (b, i, h)),
        out_shape=jax.ShapeDtypeStruct((bsz, s, d), BF16),
        scratch_shapes=[
            pltpu.VMEM((2, TK, TQ), F32),
            pltpu.VMEM((2, TK, TQ), BF16),
            pltpu.VMEM((2, 1, TQ), F32),
            pltpu.VMEM((1, TQ), F32),
            pltpu.VMEM((2, V_ROWS, TQ), F32),
        ],
        compiler_params=pltpu.CompilerParams(
            dimension_semantics=("arbitrary", "arbitrary", "arbitrary"),
            vmem_limit_bytes=VMEM_LIMIT),
        name="diff_attn",
    )(slopes, bound, q1, q2, k1, k2, v, lq1, lk1, lq2, lk2, subg)


def _row(v):
    return v.reshape(1, -1)


def _alibi_tables(n_heads):
    slopes = np.exp2(-8.0 * np.arange(1, n_heads + 1) / n_heads)
    assert np.all(np.log2(slopes) == np.round(np.log2(slopes))), "slopes must be exact in bf16"
    pos = np.arange(TM, dtype=np.float32) * np.float32(LOG2E)
    pieces = []
    for _ in range(N_POS_PIECES):
        piece = pos.astype(BF16).astype(np.float32)
        pieces.append(piece)
        pos = pos - piece
    cols = np.stack(pieces, axis=1)
    half = np.pad(cols, ((0, 0), (0, HEAD_DIM - N_POS_PIECES)))
    kext = np.tile(np.concatenate([half, half], axis=1), (1, n_heads))
    slot = (np.arange(HEAD_WIDTH) % HEAD_DIM) < N_POS_PIECES
    qext = (slopes[:, None] * slot[None, :]).reshape(-1, 1)
    return jnp.asarray(slopes, F32), jnp.asarray(kext, F32), jnp.asarray(qext, F32)


def kernel(x, ffn_norm, ffn_w_in, ffn_w_out, mix_norm, conv_w_in, conv_b_in, conv_dw, conv_dw_b, conv_ln_g, conv_ln_b, conv_w_out, conv_b_out, attn_w_qkv, attn_q_norm, attn_k_norm, attn_lq1, attn_lk1, attn_lq2, attn_lk2, attn_subln_g, attn_w_out):
    bsz, s, d = x.shape
    depth = ffn_norm.shape[0]
    n_heads = d // HEAD_WIDTH
    n_mixers = 2
    ffn_w_in = ffn_w_in.astype(BF16)
    ffn_w_out = ffn_w_out.astype(BF16)

    def ffn(x2, i, j, proj=None):
        return _ffn(x2, _row(ffn_norm[i, j]), ffn_w_in, ffn_w_out, (i, j), proj)

    x2 = x.reshape(bsz * s, d)
    for i in range(depth):
        x2 = ffn(x2, i, 0)
        j = i // n_mixers
        if i % n_mixers == 0:
            kpad = -(-CONV_WIDTH // 8) * 8
            dw3 = jnp.pad(conv_dw[j], ((0, kpad - CONV_WIDTH), (0, 0)))
            dw3 = dw3.reshape(kpad, d // LANES, LANES).transpose(1, 0, 2)
            x2 = _conv_mixer(
                x2.reshape(bsz, s, d), _row(mix_norm[i]), conv_w_in[j].astype(BF16), _row(conv_b_in[j]),
                dw3, _row(conv_dw_b[j]), _row(conv_ln_g[j]), _row(conv_ln_b[j]),
                conv_w_out[j].astype(BF16), _row(conv_b_out[j])).reshape(bsz * s, d)
            x2 = ffn(x2, i, 1)
        else:
            lambda_init = 0.8 - 0.6 * math.exp(-0.3 * i)
            gqk = _row(jnp.tile(attn_q_norm[j] * attn_k_norm[j], d // HEAD_DIM)) * (LOG2E / math.sqrt(HEAD_DIM))
            slopes, kext, qext = _alibi_tables(n_heads)
            q1, q2, k1, k2, v = _qkv(x2, _row(mix_norm[i]), attn_w_qkv[j].astype(BF16),
                                     gqk, kext, qext, bsz)
            bound = (HEAD_DIM * BOUND_MARGIN * jnp.max(jnp.abs(gqk))).reshape(1)
            o = _attention(slopes, bound, q1, q2, k1.reshape(bsz, s, d), k2.reshape(bsz, s, d), v,
                           _row(attn_lq1[j]), _row(attn_lk1[j]), _row(attn_lq2[j]), _row(attn_lk2[j]),
                           attn_subln_g[j].reshape(-1, 1), lambda_init)
            x2 = ffn(x2, i, 1, proj=(o.reshape(bsz * s, d), attn_w_out[j].astype(BF16)))
    return x2.reshape(bsz, s, d)
```

```python
import functools
import math

import jax
import jax.numpy as jnp
import numpy as np
from jax import lax
from jax.experimental import pallas as pl
from jax.experimental.pallas import tpu as pltpu

EPS = 1e-6
CONV_WIDTH = 31
HEAD_DIM = 64
HEAD_WIDTH = 2 * HEAD_DIM
LANES = 128
MXU_COLS = 256
BF16_SUBLANES = 16
V_ROWS = HEAD_WIDTH + BF16_SUBLANES
N_POS_PIECES = 3
HALO = 32
NEG_BIG = -1e30
LOG2E = 1.4426950408889634
BOUND_MARGIN = 1.02
MAX_SAFE_BOUND = 40.0

TM = 512
FFN_TM = 1024
FFN_CHUNKS = 4
TK = 512
Q_SUB = 2
TQ = Q_SUB * TK
assert TM == TK
assert Q_SUB == 2
BOUNDED_GROUP = 4
CONV_ROWS = 64
VMEM_LIMIT = 56 * 1024 * 1024

F32 = jnp.float32
BF16 = jnp.bfloat16


def _rms(x, g):
    return x * lax.rsqrt(jnp.mean(x * x, axis=-1, keepdims=True) + EPS) * g


def _resident(shape, lead=()):
    index = tuple(lead) + (0,) * len(shape)
    return pl.BlockSpec((None,) * len(lead) + tuple(shape), lambda *_: index, pipeline_mode=pl.Buffered(1))


def _ffn_chunks(f, n_chunks):
    tiles = f // MXU_COLS
    assert tiles * MXU_COLS == f
    edges = [MXU_COLS * (tiles * c // n_chunks) for c in range(n_chunks + 1)]
    return list(zip(edges[:-1], edges[1:]))


def _ffn_kernel(*refs, n_chunks, project):
    if project:
        x_ref, a_ref, wp_ref, g_ref, win_ref, wo_ref, o_ref = refs
        x = x_ref[...] + jnp.dot(a_ref[...], wp_ref[...], preferred_element_type=F32)
    else:
        x_ref, g_ref, win_ref, wo_ref, o_ref = refs
        x = x_ref[...]
    f = wo_ref.shape[0]
    h = _rms(x, g_ref[...]).astype(BF16)
    y = jnp.zeros(x.shape, F32)
    for lo, hi in _ffn_chunks(f, n_chunks):
        gate = jnp.dot(h, win_ref[:, lo:hi], preferred_element_type=F32)
        up = jnp.dot(h, win_ref[:, f + lo:f + hi], preferred_element_type=F32)
        act = (gate * jax.nn.sigmoid(gate) * up).astype(BF16)
        y = y + jnp.dot(act, wo_ref[lo:hi, :], preferred_element_type=F32)
    o_ref[...] = x + 0.5 * y


def _ffn(x2, g, w_in, w_out, which, proj=None):
    t, d = x2.shape
    f = w_out.shape[-2]
    row = pl.BlockSpec((FFN_TM, d), lambda i: (i, 0))
    proj_specs = [] if proj is None else [row, _resident((d, d))]
    proj_args = () if proj is None else proj
    return pl.pallas_call(
        functools.partial(_ffn_kernel, n_chunks=FFN_CHUNKS, project=proj is not None),
        grid=(t // FFN_TM,),
        in_specs=[row, *proj_specs, _resident((1, d)),
                  _resident((d, 2 * f), which), _resident((f, d), which)],
        out_specs=row,
        out_shape=jax.ShapeDtypeStruct((t, d), F32),
        compiler_params=pltpu.CompilerParams(
            dimension_semantics=("arbitrary",), vmem_limit_bytes=VMEM_LIMIT),
        name="ffn" if proj is None else "proj_ffn",
    )(x2, *proj_args, g, w_in, w_out)


def _conv_kernel(x_ref, g_ref, win_ref, bin_ref, dw_ref, dwb_ref,
                 lng_ref, lnb_ref, wo_ref, bo_ref, o_ref, ubuf, cbuf):
    tm, d = x_ref.shape[1], x_ref.shape[2]
    n_lane_chunks = d // LANES

    @pl.when(pl.program_id(1) == 0)
    def _():
        ubuf[:, 0:HALO, :] = jnp.zeros((n_lane_chunks, HALO, LANES), F32)

    x = x_ref[0]
    h = _rms(x, g_ref[...]).astype(BF16)

    def project(cb):
        lo, hi = cb * MXU_COLS, (cb + 1) * MXU_COLS
        a = jnp.dot(h, win_ref[:, lo:hi], preferred_element_type=F32) + bin_ref[:, lo:hi]
        b = jnp.dot(h, win_ref[:, d + lo:d + hi], preferred_element_type=F32) + bin_ref[:, d + lo:d + hi]
        return a, b

    def conv_lane_chunk(c):
        for r0 in range(0, tm, CONV_ROWS):
            acc = jnp.zeros((CONV_ROWS, LANES), F32)
            for k in range(CONV_WIDTH):
                start = HALO + r0 - (CONV_WIDTH - 1) + k
                acc = acc + dw_ref[c, k:k + 1, :] * ubuf[c, start:start + CONV_ROWS, :]
            cbuf[c, r0:r0 + CONV_ROWS, :] = acc
        ubuf[c, 0:HALO, :] = ubuf[c, tm:tm + HALO, :]

    chunks_per_block = MXU_COLS // LANES
    n_blocks = d // MXU_COLS
    ab = project(0)
    for cb in range(n_blocks):
        a, b = ab
        u = a * jax.nn.sigmoid(b)
        for s in range(chunks_per_block):
            ubuf[cb * chunks_per_block + s, HALO:HALO + tm, :] = u[:, s * LANES:(s + 1) * LANES]
        if cb + 1 < n_blocks:
            ab = project(cb + 1)
        for s in range(chunks_per_block):
            conv_lane_chunk(cb * chunks_per_block + s)

    y = jnp.concatenate([cbuf[c] for c in range(n_lane_chunks)], axis=1) + dwb_ref[...]
    mu = jnp.mean(y, axis=-1, keepdims=True)
    yc = y - mu
    var = jnp.mean(yc * yc, axis=-1, keepdims=True)
    z = yc * lax.rsqrt(var + EPS) * lng_ref[...] + lnb_ref[...]
    z = (z * jax.nn.sigmoid(z)).astype(BF16)
    m = jnp.dot(z, wo_ref[...], preferred_element_type=F32) + bo_ref[...]
    o_ref[0] = x + m


def _conv_mixer(x3, g, w_in, b_in, dw3, dwb, lng, lnb, wo, bo):
    bsz, s, d = x3.shape
    nl = d // LANES
    kpad = dw3.shape[1]
    return pl.pallas_call(
        _conv_kernel,
        grid=(bsz, s // TM),
        in_specs=[
            pl.BlockSpec((1, TM, d), lambda b, i: (b, i, 0)),
            _resident((1, d)),
            _resident((d, 2 * d)), _resident((1, 2 * d)),
            _resident((nl, kpad, LANES)),
            _resident((1, d)), _resident((1, d)), _resident((1, d)),
            _resident((d, d)), _resident((1, d)),
        ],
        out_specs=pl.BlockSpec((1, TM, d), lambda b, i: (b, i, 0)),
        out_shape=jax.ShapeDtypeStruct((bsz, s, d), F32),
        scratch_shapes=[
            pltpu.VMEM((nl, HALO + TM, LANES), F32),
            pltpu.VMEM((nl, TM, LANES), F32),
        ],
        compiler_params=pltpu.CompilerParams(
            dimension_semantics=("arbitrary", "arbitrary"), vmem_limit_bytes=VMEM_LIMIT),
        name="conv_mixer",
    )(x3, g, w_in, b_in, dw3, dwb, lng, lnb, wo, bo)


def _qkv_kernel(x_ref, g_ref, w_ref, gqk_ref, kext_ref, qext_ref,
                q1_ref, q2_ref, k1_ref, k2_ref, v_ref):
    n_heads, tm = q1_ref.shape[1], q1_ref.shape[4]
    d = x_ref.shape[1]
    x = x_ref[...]
    h = _rms(x, g_ref[...]).astype(BF16)
    q = jnp.dot(h, w_ref[:, 0:d], preferred_element_type=F32)
    k = jnp.dot(h, w_ref[:, d:2 * d], preferred_element_type=F32)
    v = jnp.dot(h, w_ref[:, 2 * d:3 * d], preferred_element_type=F32)

    def qk_norm_t(a):
        at = a.T.reshape(d // HEAD_DIM, HEAD_DIM, tm)
        at = at * lax.rsqrt(jnp.mean(at * at, axis=1, keepdims=True) + EPS)
        return at.reshape(d, tm)

    qt = qk_norm_t(q).reshape(n_heads, HEAD_WIDTH, tm)
    row_first = lax.broadcasted_iota(jnp.int32, qt.shape, 1) < HEAD_DIM
    qext = jnp.broadcast_to(qext_ref[...].reshape(n_heads, HEAD_WIDTH, 1), qt.shape)
    q1_ref[0, :, 0] = jnp.where(row_first, qt, qext).astype(BF16)
    q2_ref[0, :, 0] = jnp.where(row_first, qext, qt).astype(BF16)
    kn = qk_norm_t(k).T * gqk_ref[...]
    lane_first = (lax.broadcasted_iota(jnp.int32, kn.shape, 1) % HEAD_WIDTH) < HEAD_DIM
    kext = kext_ref[...]
    k1_ref[...] = jnp.where(lane_first, kn, kext).astype(BF16)
    k2_ref[...] = jnp.where(lane_first, kext, kn).astype(BF16)
    v_ref[0, :, 0, 0:HEAD_WIDTH, :] = v.T.reshape(n_heads, HEAD_WIDTH, tm).astype(BF16)
    ones_row = lax.broadcasted_iota(jnp.int32, (n_heads, V_ROWS - HEAD_WIDTH, tm), 1) == 0
    v_ref[0, :, 0, HEAD_WIDTH:V_ROWS, :] = jnp.where(ones_row, 1.0, 0.0).astype(BF16)


def _qkv(x2, g, w_qkv, gqk, kext, qext, bsz):
    t, d = x2.shape
    n_heads = d // HEAD_WIDTH
    n_tiles = t // bsz // TM
    row = pl.BlockSpec((TM, d), lambda i: (i, 0))
    slab = lambda rows: pl.BlockSpec((1, n_heads, 1, rows, TM), lambda i: (i // n_tiles, 0, i % n_tiles, 0, 0))
    slab_shape = lambda rows: jax.ShapeDtypeStruct((bsz, n_heads, n_tiles, rows, TM), BF16)
    rows_shape = jax.ShapeDtypeStruct((t, d), BF16)
    return pl.pallas_call(
        _qkv_kernel,
        grid=(t // TM,),
        in_specs=[row, _resident((1, d)), _resident((d, 3 * d)),
                  _resident((1, d)), _resident((TM, d)), _resident((d, 1))],
        out_specs=[slab(HEAD_WIDTH), slab(HEAD_WIDTH), row, row, slab(V_ROWS)],
        out_shape=[slab_shape(HEAD_WIDTH), slab_shape(HEAD_WIDTH), rows_shape, rows_shape, slab_shape(V_ROWS)],
        compiler_params=pltpu.CompilerParams(
            dimension_semantics=("arbitrary",), vmem_limit_bytes=VMEM_LIMIT),
        name="qkv_proj",
    )(x2, g, w_qkv, gqk, kext, qext)


def _attn_kernel(slope_ref, bound_ref, q1_ref, q2_ref, k1_ref, k2_ref, v_ref, lq1_ref, lk1_ref, lq2_ref,
                 lk2_ref, subg_ref, o_ref, z_sc, p_sc, m_sc, a_sc, acc_sc, *, lambda_init):
    hd = pl.program_id(1)
    qi = pl.program_id(2)
    slope = slope_ref[hd] * LOG2E
    tile_step = slope * TK
    first_tile = qi * Q_SUB

    qs = tuple(jnp.concatenate([q_ref[0, 0, s] for s in range(Q_SUB)], axis=1)
               for q_ref in (q1_ref, q2_ref))
    ks = (k1_ref, k2_ref)

    def scores(j, c):
        kt = ks[c][0, pl.ds(pl.multiple_of(j * TK, TK), TK), :]
        return jnp.dot(kt, qs[c], preferred_element_type=F32)

    def causal(z, key_offset):
        krow = lax.broadcasted_iota(jnp.int32, (TK, TQ), 0)
        qcol = lax.broadcasted_iota(jnp.int32, (TK, TQ), 1)
        return jnp.where(krow + key_offset <= qcol, z, NEG_BIG)

    acc_sc[...] = jnp.zeros(acc_sc.shape, F32)

    def bounded_tiles(j0, n_tiles, with_diagonal):
        qoff = lax.broadcasted_iota(jnp.int32, (1, TQ), 1).astype(F32)
        ref = bound_ref[0] + slope * qoff
        n_wide = n_tiles - 1 if with_diagonal else n_tiles
        values = lambda j: v_ref[0, 0, j, 0:HEAD_WIDTH, :]
        for c in range(2):
            ps = []
            denom = jnp.zeros((1, TQ), F32)
            for t in range(n_wide):
                z = scores(j0 + t, c)
                if with_diagonal and t == n_wide - 1:
                    z = causal(z, 0)
                off = tile_step * (j0 + t - first_tile).astype(F32) - ref
                p = jnp.exp2(z + off)
                denom = denom + jnp.sum(p, axis=0, keepdims=True)
                ps.append(p.astype(BF16))
            vt = jnp.concatenate([values(j0 + t) for t in range(n_wide)], axis=1)
            acc_sc[c, 0:HEAD_WIDTH, :] += jnp.dot(vt, jnp.concatenate(ps, axis=0), preferred_element_type=F32)
            acc_sc[c, HEAD_WIDTH:HEAD_WIDTH + 1, :] += denom
            if with_diagonal:
                jl = j0 + n_wide
                kt = ks[c][0, pl.ds(pl.multiple_of(jl * TK, TK), TK), :]
                z = jnp.dot(kt, qs[c][:, TK:], preferred_element_type=F32)
                krow = lax.broadcasted_iota(jnp.int32, (TK, TK), 0)
                qcol = lax.broadcasted_iota(jnp.int32, (TK, TK), 1)
                z = jnp.where(krow <= qcol, z, NEG_BIG)
                off = tile_step * (jl - first_tile).astype(F32) - ref[:, TK:]
                p = jnp.exp2(z + off)
                acc_sc[c, HEAD_WIDTH:HEAD_WIDTH + 1, TK:] += jnp.sum(p, axis=0, keepdims=True)
                acc_sc[c, 0:HEAD_WIDTH, TK:] += jnp.dot(values(jl), p.astype(BF16), preferred_element_type=F32)

    def bounded_path():
        def group(t, carry):
            bounded_tiles(BOUNDED_GROUP * t, BOUNDED_GROUP, with_diagonal=False)
            return carry

        lax.fori_loop(0, first_tile // BOUNDED_GROUP, group, 0)
        for rest in range(0, BOUNDED_GROUP, Q_SUB):
            pl.when(first_tile % BOUNDED_GROUP == rest)(
                functools.partial(bounded_tiles, first_tile - rest, rest + Q_SUB, with_diagonal=True))

    def softmax(z, j, c, key_offset):
        if key_offset is not None:
            z = causal(z, key_offset)
        shift = tile_step * (j - first_tile).astype(F32)
        m_old = m_sc[c]
        m_new = jnp.maximum(m_old, jnp.max(z, axis=0, keepdims=True) + shift)
        m_sc[c] = m_new
        return jnp.exp2(z - (m_new - shift)).astype(BF16), jnp.exp2(m_old - m_new)

    def accumulate(c, j, p, alpha):
        acc_sc[c] = alpha * acc_sc[c] + jnp.dot(v_ref[0, 0, j], p, preferred_element_type=F32)

    def step(j, par, key_offset=None, last=False):
        zb = scores(j, 1)
        pa, alpha_a = softmax(z_sc[par], j, 0, key_offset)
        accumulate(1, jnp.maximum(j - 1, 0), p_sc[1 - par], a_sc[...])
        if not last:
            z_sc[1 - par] = scores(j + 1, 0)
        pb, alpha_b = softmax(zb, j, 1, key_offset)
        accumulate(0, j, pa, alpha_a)
        if last:
            accumulate(1, j, pb, alpha_b)
        else:
            p_sc[par] = pb
            a_sc[...] = alpha_b

    def running_max_path():
        m_sc[...] = jnp.full(m_sc.shape, NEG_BIG, F32)
        a_sc[...] = jnp.ones(a_sc.shape, F32)
        p_sc[1] = jnp.zeros(p_sc.shape[1:], BF16)
        z_sc[0] = scores(0, 0)

        def pair(t, carry):
            step(2 * t, 0)
            step(2 * t + 1, 1)
            return carry

        lax.fori_loop(0, first_tile // 2, pair, 0)
        for s in range(Q_SUB):
            step(first_tile + s, s % 2, key_offset=s * TK, last=s == Q_SUB - 1)

    bounded = bound_ref[0] <= MAX_SAFE_BOUND
    pl.when(bounded)(bounded_path)
    pl.when(jnp.logical_not(bounded))(running_max_path)

    lam = (jnp.exp(jnp.sum(lq1_ref[...] * lk1_ref[...], keepdims=True))
           - jnp.exp(jnp.sum(lq2_ref[...] * lk2_ref[...], keepdims=True))
           + lambda_init)
    acc_a, acc_b = acc_sc[0], acc_sc[1]
    ot = (acc_a[:HEAD_WIDTH] / acc_a[HEAD_WIDTH:HEAD_WIDTH + 1]
          - lam * (acc_b[:HEAD_WIDTH] / acc_b[HEAD_WIDTH:HEAD_WIDTH + 1]))
    ot = ot * lax.rsqrt(jnp.mean(ot * ot, axis=0, keepdims=True) + EPS) * subg_ref[...]
    o_ref[0] = (ot * (1.0 - lambda_init)).T.astype(BF16)


def _attention(slopes, bound, q1, q2, k1, k2, v, lq1, lk1, lq2, lk2, subg, lambda_init):
    bsz, n_heads, n_tiles = q1.shape[:3]
    s, d = k1.shape[1], k1.shape[2]
    qspec = pl.BlockSpec((1, 1, Q_SUB, HEAD_WIDTH, TK), lambda b, h, i: (b, h, i, 0, 0))
    kspec = pl.BlockSpec((1, s, HEAD_WIDTH), lambda b, h, i: (b, 0, h))
    vspec = pl.BlockSpec((1, 1, n_tiles, V_ROWS, TK), lambda b, h, i: (b, h, 0, 0, 0))
    small = lambda n: pl.BlockSpec((1, n), lambda b, h, i: (0, 0))
    return pl.pallas_call(
        functools.partial(_attn_kernel, lambda_init=lambda_init),
        grid=(bsz, n_heads, s // TQ),
        in_specs=[
            pl.BlockSpec(memory_space=pltpu.SMEM), pl.BlockSpec(memory_space=pltpu.SMEM),
            qspec, qspec, kspec, kspec, vspec,
            small(HEAD_DIM), small(HEAD_DIM), small(HEAD_DIM), small(HEAD_DIM),
            pl.BlockSpec((HEAD_WIDTH, 1), lambda b, h, i: (0, 0)),
        ],
        out_specs=pl.BlockSpec((1, TQ, HEAD_WIDTH), lambda b, h, i: (b, i, h)),
        out_shape=jax.ShapeDtypeStruct((bsz, s, d), BF16),
        scratch_shapes=[
            pltpu.VMEM((2, TK, TQ), F32),
            pltpu.VMEM((2, TK, TQ), BF16),
            pltpu.VMEM((2, 1, TQ), F32),
            pltpu.VMEM((1, TQ), F32),
            pltpu.VMEM((2, V_ROWS, TQ), F32),
        ],
        compiler_params=pltpu.CompilerParams(
            dimension_semantics=("arbitrary", "arbitrary", "arbitrary"),
            vmem_limit_bytes=VMEM_LIMIT),
        name="diff_attn",
    )(slopes, bound, q1, q2, k1, k2, v, lq1, lk1, lq2, lk2, subg)


def _row(v):
    return v.reshape(1, -1)


def _alibi_tables(n_heads):
    slopes = np.exp2(-8.0 * np.arange(1, n_heads + 1) / n_heads)
    assert np.all(np.log2(slopes) == np.round(np.log2(slopes))), "slopes must be exact in bf16"
    pos = np.arange(TM, dtype=np.float32) * np.float32(LOG2E)
    pieces = []
    for _ in range(N_POS_PIECES):
        piece = pos.astype(BF16).astype(np.float32)
        pieces.append(piece)
        pos = pos - piece
    cols = np.stack(pieces, axis=1)
    half = np.pad(cols, ((0, 0), (0, HEAD_DIM - N_POS_PIECES)))
    kext = np.tile(np.concatenate([half, half], axis=1), (1, n_heads))
    slot = (np.arange(HEAD_WIDTH) % HEAD_DIM) < N_POS_PIECES
    qext = (slopes[:, None] * slot[None, :]).reshape(-1, 1)
    return jnp.asarray(slopes, F32), jnp.asarray(kext, F32), jnp.asarray(qext, F32)


def kernel(x, ffn_norm, ffn_w_in, ffn_w_out, mix_norm, conv_w_in, conv_b_in, conv_dw, conv_dw_b, conv_ln_g, conv_ln_b, conv_w_out, conv_b_out, attn_w_qkv, attn_q_norm, attn_k_norm, attn_lq1, attn_lk1, attn_lq2, attn_lk2, attn_subln_g, attn_w_out):
    bsz, s, d = x.shape
    depth = ffn_norm.shape[0]
    n_heads = d // HEAD_WIDTH
    n_mixers = 2
    ffn_w_in = ffn_w_in.astype(BF16)
    ffn_w_out = ffn_w_out.astype(BF16)

    def ffn(x2, i, j, proj=None):
        return _ffn(x2, _row(ffn_norm[i, j]), ffn_w_in, ffn_w_out, (i, j), proj)

    x2 = x.reshape(bsz * s, d)
    for i in range(depth):
        x2 = ffn(x2, i, 0)
        j = i // n_mixers
        if i % n_mixers == 0:
            kpad = -(-CONV_WIDTH // 8) * 8
            dw3 = jnp.pad(conv_dw[j], ((0, kpad - CONV_WIDTH), (0, 0)))
            dw3 = dw3.reshape(kpad, d // LANES, LANES).transpose(1, 0, 2)
            x2 = _conv_mixer(
                x2.reshape(bsz, s, d), _row(mix_norm[i]), conv_w_in[j].astype(BF16), _row(conv_b_in[j]),
                dw3, _row(conv_dw_b[j]), _row(conv_ln_g[j]), _row(conv_ln_b[j]),
                conv_w_out[j].astype(BF16), _row(conv_b_out[j])).reshape(bsz * s, d)
            x2 = ffn(x2, i, 1)
        else:
            lambda_init = 0.8 - 0.6 * math.exp(-0.3 * i)
            gqk = _row(jnp.tile(attn_q_norm[j] * attn_k_norm[j], d // HEAD_DIM)) * (LOG2E / math.sqrt(HEAD_DIM))
            slopes, kext, qext = _alibi_tables(n_heads)
            q1, q2, k1, k2, v = _qkv(x2, _row(mix_norm[i]), attn_w_qkv[j].astype(BF16),
                                     gqk, kext, qext, bsz)
            bound = (HEAD_DIM * BOUND_MARGIN * jnp.max(jnp.abs(gqk))).reshape(1)
            o = _attention(slopes, bound, q1, q2, k1.reshape(bsz, s, d), k2.reshape(bsz, s, d), v,
                           _row(attn_lq1[j]), _row(attn_lk1[j]), _row(attn_lq2[j]), _row(attn_lk2[j]),
                           attn_subln_g[j].reshape(-1, 1), lambda_init)
            x2 = ffn(x2, i, 1, proj=(o.reshape(bsz * s, d), attn_w_out[j].astype(BF16)))
    return x2.reshape(bsz, s, d)
```

```python
import functools
import math

import jax
import jax.numpy as jnp
import numpy as np
from jax import lax
from jax.experimental import pallas as pl
from jax.experimental.pallas import tpu as pltpu

EPS = 1e-6
CONV_WIDTH = 31
HEAD_DIM = 64
HEAD_WIDTH = 2 * HEAD_DIM
LANES = 128
MXU_COLS = 256
BF16_SUBLANES = 16
V_ROWS = HEAD_WIDTH + BF16_SUBLANES
N_POS_PIECES = 3
HALO = 32
NEG_BIG = -1e30
LOG2E = 1.4426950408889634
BOUND_MARGIN = 1.02
MAX_SAFE_BOUND = 40.0

TM = 512
FFN_TM = 1024
FFN_CHUNKS = 4
STAGE_CHUNKS = 16
TK = 512
Q_SUB = 2
TQ = Q_SUB * TK
assert TM == TK
assert Q_SUB == 2
BOUNDED_GROUP = 4
CONV_ROWS = 64
VMEM_LIMIT = 56 * 1024 * 1024

F32 = jnp.float32
BF16 = jnp.bfloat16


def _rms(x, g):
    return x * lax.rsqrt(jnp.mean(x * x, axis=-1, keepdims=True) + EPS) * g


def _resident(shape, lead=()):
    index = tuple(lead) + (0,) * len(shape)
    return pl.BlockSpec((None,) * len(lead) + tuple(shape), lambda *_: index, pipeline_mode=pl.Buffered(1))


def _ffn_chunks(f, n_chunks):
    tiles = f // MXU_COLS
    assert tiles * MXU_COLS == f
    edges = [MXU_COLS * (tiles * c // n_chunks) for c in range(n_chunks + 1)]
    return list(zip(edges[:-1], edges[1:]))


def _stage_bf16(src_hbm, which, dst):
    n_rows, n_cols = dst.shape
    rows = n_rows // STAGE_CHUNKS
    assert rows * STAGE_CHUNKS == n_rows and rows % BF16_SUBLANES == 0

    def body(stage, sem):
        def copy(k):
            return pltpu.make_async_copy(
                src_hbm.at[which[0], which[1], pl.ds(k * rows, rows), :], stage.at[k % 2], sem.at[k % 2])

        copy(0).start()
        for k in range(STAGE_CHUNKS):
            if k + 1 < STAGE_CHUNKS:
                copy(k + 1).start()
            copy(k).wait()
            dst[k * rows:(k + 1) * rows, :] = stage[k % 2].astype(BF16)

    pl.run_scoped(body, pltpu.VMEM((2, rows, n_cols), F32), pltpu.SemaphoreType.DMA((2,)))


def _ffn_kernel(*refs, n_chunks, project, which):
    if project:
        x_ref, a_ref, wp_ref, g_ref, win_hbm, wo_hbm, o_ref, win_ref, wo_ref = refs
    else:
        x_ref, g_ref, win_hbm, wo_hbm, o_ref, win_ref, wo_ref = refs

    @pl.when(pl.program_id(0) == 0)
    def _():
        _stage_bf16(win_hbm, which, win_ref)
        _stage_bf16(wo_hbm, which, wo_ref)

    if project:
        x = x_ref[...] + jnp.dot(a_ref[...], wp_ref[...], preferred_element_type=F32)
    else:
        x = x_ref[...]
    f = wo_ref.shape[0]
    h = _rms(x, g_ref[...]).astype(BF16)
    y = jnp.zeros(x.shape, F32)
    for lo, hi in _ffn_chunks(f, n_chunks):
        gate = jnp.dot(h, win_ref[:, lo:hi], preferred_element_type=F32)
        up = jnp.dot(h, win_ref[:, f + lo:f + hi], preferred_element_type=F32)
        act = (gate * jax.nn.sigmoid(gate) * up).astype(BF16)
        y = y + jnp.dot(act, wo_ref[lo:hi, :], preferred_element_type=F32)
    o_ref[...] = x + 0.5 * y


def _ffn(x2, g, w_in, w_out, which, proj=None):
    t, d = x2.shape
    f = w_out.shape[-2]
    row = pl.BlockSpec((FFN_TM, d), lambda i: (i, 0))
    proj_specs = [] if proj is None else [row, _resident((d, d))]
    proj_args = () if proj is None else proj
    return pl.pallas_call(
        functools.partial(_ffn_kernel, n_chunks=FFN_CHUNKS, project=proj is not None, which=which),
        grid=(t // FFN_TM,),
        in_specs=[row, *proj_specs, _resident((1, d)),
                  pl.BlockSpec(memory_space=pl.ANY), pl.BlockSpec(memory_space=pl.ANY)],
        out_specs=row,
        out_shape=jax.ShapeDtypeStruct((t, d), F32),
        scratch_shapes=[pltpu.VMEM((d, 2 * f), BF16), pltpu.VMEM((f, d), BF16)],
        compiler_params=pltpu.CompilerParams(
            dimension_semantics=("arbitrary",), vmem_limit_bytes=VMEM_LIMIT),
        name="ffn" if proj is None else "proj_ffn",
    )(x2, *proj_args, g, w_in, w_out)


def _conv_kernel(x_ref, g_ref, win_ref, bin_ref, dw_ref, dwb_ref,
                 lng_ref, lnb_ref, wo_ref, bo_ref, o_ref, ubuf, cbuf):
    tm, d = x_ref.shape[1], x_ref.shape[2]
    n_lane_chunks = d // LANES

    @pl.when(pl.program_id(1) == 0)
    def _():
        ubuf[:, 0:HALO, :] = jnp.zeros((n_lane_chunks, HALO, LANES), F32)

    x = x_ref[0]
    h = _rms(x, g_ref[...]).astype(BF16)

    def project(cb):
        lo, hi = cb * MXU_COLS, (cb + 1) * MXU_COLS
        a = jnp.dot(h, win_ref[:, lo:hi], preferred_element_type=F32) + bin_ref[:, lo:hi]
        b = jnp.dot(h, win_ref[:, d + lo:d + hi], preferred_element_type=F32) + bin_ref[:, d + lo:d + hi]
        return a, b

    def conv_lane_chunk(c):
        for r0 in range(0, tm, CONV_ROWS):
            acc = jnp.zeros((CONV_ROWS, LANES), F32)
            for k in range(CONV_WIDTH):
                start = HALO + r0 - (CONV_WIDTH - 1) + k
                acc = acc + dw_ref[c, k:k + 1, :] * ubuf[c, start:start + CONV_ROWS, :]
            cbuf[c, r0:r0 + CONV_ROWS, :] = acc
        ubuf[c, 0:HALO, :] = ubuf[c, tm:tm + HALO, :]

    chunks_per_block = MXU_COLS // LANES
    n_blocks = d // MXU_COLS
    ab = project(0)
    for cb in range(n_blocks):
        a, b = ab
        u = a * jax.nn.sigmoid(b)
        for s in range(chunks_per_block):
            ubuf[cb * chunks_per_block + s, HALO:HALO + tm, :] = u[:, s * LANES:(s + 1) * LANES]
        if cb + 1 < n_blocks:
            ab = project(cb + 1)
        for s in range(chunks_per_block):
            conv_lane_chunk(cb * chunks_per_block + s)

    y = jnp.concatenate([cbuf[c] for c in range(n_lane_chunks)], axis=1) + dwb_ref[...]
    mu = jnp.mean(y, axis=-1, keepdims=True)
    yc = y - mu
    var = jnp.mean(yc * yc, axis=-1, keepdims=True)
    z = yc * lax.rsqrt(var + EPS) * lng_ref[...] + lnb_ref[...]
    z = (z * jax.nn.sigmoid(z)).astype(BF16)
    m = jnp.dot(z, wo_ref[...], preferred_element_type=F32) + bo_ref[...]
    o_ref[0] = x + m


def _conv_mixer(x3, g, w_in, b_in, dw3, dwb, lng, lnb, wo, bo):
    bsz, s, d = x3.shape
    nl = d // LANES
    kpad = dw3.shape[1]
    return pl.pallas_call(
        _conv_kernel,
        grid=(bsz, s // TM),
        in_specs=[
            pl.BlockSpec((1, TM, d), lambda b, i: (b, i, 0)),
            _resident((1, d)),
            _resident((d, 2 * d)), _resident((1, 2 * d)),
            _resident((nl, kpad, LANES)),
            _resident((1, d)), _resident((1, d)), _resident((1, d)),
            _resident((d, d)), _resident((1, d)),
        ],
        out_specs=pl.BlockSpec((1, TM, d), lambda b, i: (b, i, 0)),
        out_shape=jax.ShapeDtypeStruct((bsz, s, d), F32),
        scratch_shapes=[
            pltpu.VMEM((nl, HALO + TM, LANES), F32),
            pltpu.VMEM((nl, TM, LANES), F32),
        ],
        compiler_params=pltpu.CompilerParams(
            dimension_semantics=("arbitrary", "arbitrary"), vmem_limit_bytes=VMEM_LIMIT),
        name="conv_mixer",
    )(x3, g, w_in, b_in, dw3, dwb, lng, lnb, wo, bo)


def _qkv_kernel(x_ref, g_ref, w_ref, gqk_ref, kext_ref, qext_ref,
                q1_ref, q2_ref, k1_ref, k2_ref, v_ref):
    n_heads, tm = q1_ref.shape[1], q1_ref.shape[4]
    d = x_ref.shape[1]
    x = x_ref[...]
    h = _rms(x, g_ref[...]).astype(BF16)
    q = jnp.dot(h, w_ref[:, 0:d], preferred_element_type=F32)
    k = jnp.dot(h, w_ref[:, d:2 * d], preferred_element_type=F32)
    v = jnp.dot(h, w_ref[:, 2 * d:3 * d], preferred_element_type=F32)

    def qk_norm_t(a):
        at = a.T.reshape(d // HEAD_DIM, HEAD_DIM, tm)
        at = at * lax.rsqrt(jnp.mean(at * at, axis=1, keepdims=True) + EPS)
        return at.reshape(d, tm)

    qt = qk_norm_t(q).reshape(n_heads, HEAD_WIDTH, tm)
    row_first = lax.broadcasted_iota(jnp.int32, qt.shape, 1) < HEAD_DIM
    qext = jnp.broadcast_to(qext_ref[...].reshape(n_heads, HEAD_WIDTH, 1), qt.shape)
    q1_ref[0, :, 0] = jnp.where(row_first, qt, qext).astype(BF16)
    q2_ref[0, :, 0] = jnp.where(row_first, qext, qt).astype(BF16)
    kn = qk_norm_t(k).T * gqk_ref[...]
    lane_first = (lax.broadcasted_iota(jnp.int32, kn.shape, 1) % HEAD_WIDTH) < HEAD_DIM
    kext = kext_ref[...]
    k1_ref[...] = jnp.where(lane_first, kn, kext).astype(BF16)
    k2_ref[...] = jnp.where(lane_first, kext, kn).astype(BF16)
    v_ref[0, :, 0, 0:HEAD_WIDTH, :] = v.T.reshape(n_heads, HEAD_WIDTH, tm).astype(BF16)
    ones_row = lax.broadcasted_iota(jnp.int32, (n_heads, V_ROWS - HEAD_WIDTH, tm), 1) == 0
    v_ref[0, :, 0, HEAD_WIDTH:V_ROWS, :] = jnp.where(ones_row, 1.0, 0.0).astype(BF16)


def _qkv(x2, g, w_qkv, gqk, kext, qext, bsz):
    t, d = x2.shape
    n_heads = d // HEAD_WIDTH
    n_tiles = t // bsz // TM
    row = pl.BlockSpec((TM, d), lambda i: (i, 0))
    slab = lambda rows: pl.BlockSpec((1, n_heads, 1, rows, TM), lambda i: (i // n_tiles, 0, i % n_tiles, 0, 0))
    slab_shape = lambda rows: jax.ShapeDtypeStruct((bsz, n_heads, n_tiles, rows, TM), BF16)
    rows_shape = jax.ShapeDtypeStruct((t, d), BF16)
    return pl.pallas_call(
        _qkv_kernel,
        grid=(t // TM,),
        in_specs=[row, _resident((1, d)), _resident((d, 3 * d)),
                  _resident((1, d)), _resident((TM, d)), _resident((d, 1))],
        out_specs=[slab(HEAD_WIDTH), slab(HEAD_WIDTH), row, row, slab(V_ROWS)],
        out_shape=[slab_shape(HEAD_WIDTH), slab_shape(HEAD_WIDTH), rows_shape, rows_shape, slab_shape(V_ROWS)],
        compiler_params=pltpu.CompilerParams(
            dimension_semantics=("arbitrary",), vmem_limit_bytes=VMEM_LIMIT),
        name="qkv_proj",
    )(x2, g, w_qkv, gqk, kext, qext)


def _attn_kernel(slope_ref, bound_ref, q1_ref, q2_ref, k1_ref, k2_ref, v_ref, lq1_ref, lk1_ref, lq2_ref,
                 lk2_ref, subg_ref, o_ref, z_sc, p_sc, m_sc, a_sc, acc_sc, *, lambda_init):
    hd = pl.program_id(1)
    qi = pl.program_id(2)
    slope = slope_ref[hd] * LOG2E
    tile_step = slope * TK
    first_tile = qi * Q_SUB

    qs = tuple(jnp.concatenate([q_ref[0, 0, s] for s in range(Q_SUB)], axis=1)
               for q_ref in (q1_ref, q2_ref))
    ks = (k1_ref, k2_ref)

    def scores(j, c):
        kt = ks[c][0, pl.ds(pl.multiple_of(j * TK, TK), TK), :]
        return jnp.dot(kt, qs[c], preferred_element_type=F32)

    def causal(z, key_offset):
        krow = lax.broadcasted_iota(jnp.int32, (TK, TQ), 0)
        qcol = lax.broadcasted_iota(jnp.int32, (TK, TQ), 1)
        return jnp.where(krow + key_offset <= qcol, z, NEG_BIG)

    acc_sc[...] = jnp.zeros(acc_sc.shape, F32)

    def bounded_tiles(j0, n_tiles, with_diagonal):
        qoff = lax.broadcasted_iota(jnp.int32, (1, TQ), 1).astype(F32)
        ref = bound_ref[0] + slope * qoff
        n_wide = n_tiles - 1 if with_diagonal else n_tiles
        values = lambda j: v_ref[0, 0, j, 0:HEAD_WIDTH, :]
        for c in range(2):
            ps = []
            denom = jnp.zeros((1, TQ), F32)
            for t in range(n_wide):
                z = scores(j0 + t, c)
                if with_diagonal and t == n_wide - 1:
                    z = causal(z, 0)
                off = tile_step * (j0 + t - first_tile).astype(F32) - ref
                p = jnp.exp2(z + off)
                denom = denom + jnp.sum(p, axis=0, keepdims=True)
                ps.append(p.astype(BF16))
            vt = jnp.concatenate([values(j0 + t) for t in range(n_wide)], axis=1)
            acc_sc[c, 0:HEAD_WIDTH, :] += jnp.dot(vt, jnp.concatenate(ps, axis=0), preferred_element_type=F32)
            acc_sc[c, HEAD_WIDTH:HEAD_WIDTH + 1, :] += denom
            if with_diagonal:
                jl = j0 + n_wide
                kt = ks[c][0, pl.ds(pl.multiple_of(jl * TK, TK), TK), :]
                z = jnp.dot(kt, qs[c][:, TK:], preferred_element_type=F32)
                krow = lax.broadcasted_iota(jnp.int32, (TK, TK), 0)
                qcol = lax.broadcasted_iota(jnp.int32, (TK, TK), 1)
                z = jnp.where(krow <= qcol, z, NEG_BIG)
                off = tile_step * (jl - first_tile).astype(F32) - ref[:, TK:]
                p = jnp.exp2(z + off)
                acc_sc[c, HEAD_WIDTH:HEAD_WIDTH + 1, TK:] += jnp.sum(p, axis=0, keepdims=True)
                acc_sc[c, 0:HEAD_WIDTH, TK:] += jnp.dot(values(jl), p.astype(BF16), preferred_element_type=F32)

    def bounded_path():
        def group(t, carry):
            bounded_tiles(BOUNDED_GROUP * t, BOUNDED_GROUP, with_diagonal=False)
            return carry

        lax.fori_loop(0, first_tile // BOUNDED_GROUP, group, 0)
        for rest in range(0, BOUNDED_GROUP, Q_SUB):
            pl.when(first_tile % BOUNDED_GROUP == rest)(
                functools.partial(bounded_tiles, first_tile - rest, rest + Q_SUB, with_diagonal=True))

    def softmax(z, j, c, key_offset):
        if key_offset is not None:
            z = causal(z, key_offset)
        shift = tile_step * (j - first_tile).astype(F32)
        m_old = m_sc[c]
        m_new = jnp.maximum(m_old, jnp.max(z, axis=0, keepdims=True) + shift)
        m_sc[c] = m_new
        return jnp.exp2(z - (m_new - shift)).astype(BF16), jnp.exp2(m_old - m_new)

    def accumulate(c, j, p, alpha):
        acc_sc[c] = alpha * acc_sc[c] + jnp.dot(v_ref[0, 0, j], p, preferred_element_type=F32)

    def step(j, par, key_offset=None, last=False):
        zb = scores(j, 1)
        pa, alpha_a = softmax(z_sc[par], j, 0, key_offset)
        accumulate(1, jnp.maximum(j - 1, 0), p_sc[1 - par], a_sc[...])
        if not last:
            z_sc[1 - par] = scores(j + 1, 0)
        pb, alpha_b = softmax(zb, j, 1, key_offset)
        accumulate(0, j, pa, alpha_a)
        if last:
            accumulate(1, j, pb, alpha_b)
        else:
            p_sc[par] = pb
            a_sc[...] = alpha_b

    def running_max_path():
        m_sc[...] = jnp.full(m_sc.shape, NEG_BIG, F32)
        a_sc[...] = jnp.ones(a_sc.shape, F32)
        p_sc[1] = jnp.zeros(p_sc.shape[1:], BF16)
        z_sc[0] = scores(0, 0)

        def pair(t, carry):
            step(2 * t, 0)
            step(2 * t + 1, 1)
            return carry

        lax.fori_loop(0, first_tile // 2, pair, 0)
        for s in range(Q_SUB):
            step(first_tile + s, s % 2, key_offset=s * TK, last=s == Q_SUB - 1)

    bounded = bound_ref[0] <= MAX_SAFE_BOUND
    pl.when(bounded)(bounded_path)
    pl.when(jnp.logical_not(bounded))(running_max_path)

    lam = (jnp.exp(jnp.sum(lq1_ref[...] * lk1_ref[...], keepdims=True))
           - jnp.exp(jnp.sum(lq2_ref[...] * lk2_ref[...], keepdims=True))
           + lambda_init)
    acc_a, acc_b = acc_sc[0], acc_sc[1]
    ot = (acc_a[:HEAD_WIDTH] / acc_a[HEAD_WIDTH:HEAD_WIDTH + 1]
          - lam * (acc_b[:HEAD_WIDTH] / acc_b[HEAD_WIDTH:HEAD_WIDTH + 1]))
    ot = ot * lax.rsqrt(jnp.mean(ot * ot, axis=0, keepdims=True) + EPS) * subg_ref[...]
    o_ref[0] = (ot * (1.0 - lambda_init)).T.astype(BF16)


def _attention(slopes, bound, q1, q2, k1, k2, v, lq1, lk1, lq2, lk2, subg, lambda_init):
    bsz, n_heads, n_tiles = q1.shape[:3]
    s, d = k1.shape[1], k1.shape[2]
    qspec = pl.BlockSpec((1, 1, Q_SUB, HEAD_WIDTH, TK), lambda b, h, i: (b, h, i, 0, 0))
    kspec = pl.BlockSpec((1, s, HEAD_WIDTH), lambda b, h, i: (b, 0, h))
    vspec = pl.BlockSpec((1, 1, n_tiles, V_ROWS, TK), lambda b, h, i: (b, h, 0, 0, 0))
    small = lambda n: pl.BlockSpec((1, n), lambda b, h, i: (0, 0))
    return pl.pallas_call(
        functools.partial(_attn_kernel, lambda_init=lambda_init),
        grid=(bsz, n_heads, s // TQ),
        in_specs=[
            pl.BlockSpec(memory_space=pltpu.SMEM), pl.BlockSpec(memory_space=pltpu.SMEM),
            qspec, qspec, kspec, kspec, vspec,
            small(HEAD_DIM), small(HEAD_DIM), small(HEAD_DIM), small(HEAD_DIM),
            pl.BlockSpec((HEAD_WIDTH, 1), lambda b, h, i: (0, 0)),
        ],
        out_specs=pl.BlockSpec((1, TQ, HEAD_WIDTH), lambda b, h, i: (b, i, h)),
        out_shape=jax.ShapeDtypeStruct((bsz, s, d), BF16),
        scratch_shapes=[
            pltpu.VMEM((2, TK, TQ), F32),
            pltpu.VMEM((2, TK, TQ), BF16),
            pltpu.VMEM((2, 1, TQ), F32),
            pltpu.VMEM((1, TQ), F32),
            pltpu.VMEM((2, V_ROWS, TQ), F32),
        ],
        compiler_params=pltpu.CompilerParams(
            dimension_semantics=("arbitrary", "arbitrary", "arbitrary"),
            vmem_limit_bytes=VMEM_LIMIT),
        name="diff_attn",
    )(slopes, bound, q1, q2, k1, k2, v, lq1, lk1, lq2, lk2, subg)


def _row(v):
    return v.reshape(1, -1)


def _alibi_tables(n_heads):
    slopes = np.exp2(-8.0 * np.arange(1, n_heads + 1) / n_heads)
    assert np.all(np.log2(slopes) == np.round(np.log2(slopes))), "slopes must be exact in bf16"
    pos = np.arange(TM, dtype=np.float32) * np.float32(LOG2E)
    pieces = []
    for _ in range(N_POS_PIECES):
        piece = pos.astype(BF16).astype(np.float32)
        pieces.append(piece)
        pos = pos - piece
    cols = np.stack(pieces, axis=1)
    half = np.pad(cols, ((0, 0), (0, HEAD_DIM - N_POS_PIECES)))
    kext = np.tile(np.concatenate([half, half], axis=1), (1, n_heads))
    slot = (np.arange(HEAD_WIDTH) % HEAD_DIM) < N_POS_PIECES
    qext = (slopes[:, None] * slot[None, :]).reshape(-1, 1)
    return jnp.asarray(slopes, F32), jnp.asarray(kext, F32), jnp.asarray(qext, F32)


def kernel(x, ffn_norm, ffn_w_in, ffn_w_out, mix_norm, conv_w_in, conv_b_in, conv_dw, conv_dw_b, conv_ln_g, conv_ln_b, conv_w_out, conv_b_out, attn_w_qkv, attn_q_norm, attn_k_norm, attn_lq1, attn_lk1, attn_lq2, attn_lk2, attn_subln_g, attn_w_out):
    bsz, s, d = x.shape
    depth = ffn_norm.shape[0]
    n_heads = d // HEAD_WIDTH
    n_mixers = 2

    def ffn(x2, i, j, proj=None):
        return _ffn(x2, _row(ffn_norm[i, j]), ffn_w_in, ffn_w_out, (i, j), proj)

    x2 = x.reshape(bsz * s, d)
    for i in range(depth):
        x2 = ffn(x2, i, 0)
        j = i // n_mixers
        if i % n_mixers == 0:
            kpad = -(-CONV_WIDTH // 8) * 8
            dw3 = jnp.pad(conv_dw[j], ((0, kpad - CONV_WIDTH), (0, 0)))
            dw3 = dw3.reshape(kpad, d // LANES, LANES).transpose(1, 0, 2)
            x2 = _conv_mixer(
                x2.reshape(bsz, s, d), _row(mix_norm[i]), conv_w_in[j].astype(BF16), _row(conv_b_in[j]),
                dw3, _row(conv_dw_b[j]), _row(conv_ln_g[j]), _row(conv_ln_b[j]),
                conv_w_out[j].astype(BF16), _row(conv_b_out[j])).reshape(bsz * s, d)
            x2 = ffn(x2, i, 1)
        else:
            lambda_init = 0.8 - 0.6 * math.exp(-0.3 * i)
            gqk = _row(jnp.tile(attn_q_norm[j] * attn_k_norm[j], d // HEAD_DIM)) * (LOG2E / math.sqrt(HEAD_DIM))
            slopes, kext, qext = _alibi_tables(n_heads)
            q1, q2, k1, k2, v = _qkv(x2, _row(mix_norm[i]), attn_w_qkv[j].astype(BF16),
                                     gqk, kext, qext, bsz)
            bound = (HEAD_DIM * BOUND_MARGIN * jnp.max(jnp.abs(gqk))).reshape(1)
            o = _attention(slopes, bound, q1, q2, k1.reshape(bsz, s, d), k2.reshape(bsz, s, d), v,
                           _row(attn_lq1[j]), _row(attn_lk1[j]), _row(attn_lq2[j]), _row(attn_lk2[j]),
                           attn_subln_g[j].reshape(-1, 1), lambda_init)
            x2 = ffn(x2, i, 1, proj=(o.reshape(bsz * s, d), attn_w_out[j].astype(BF16)))
    return x2.reshape(bsz, s, d)
```

```python
import functools
import math

import jax
import jax.numpy as jnp
import numpy as np
from jax import lax
from jax.experimental import pallas as pl
from jax.experimental.pallas import tpu as pltpu

EPS = 1e-6
CONV_WIDTH = 31
HEAD_DIM = 64
HEAD_WIDTH = 2 * HEAD_DIM
LANES = 128
MXU_COLS = 256
BF16_SUBLANES = 16
V_ROWS = HEAD_WIDTH + BF16_SUBLANES
N_POS_PIECES = 3
HALO = 32
NEG_BIG = -1e30
LOG2E = 1.4426950408889634
BOUND_MARGIN = 1.02
MAX_SAFE_BOUND = 40.0

TM = 512
FFN_TM = 1024
FFN_CHUNKS = 4
STAGE_CHUNK_BYTES = 768 * 1024
STAGE_SLOTS = 5
TK = 512
Q_SUB = 2
TQ = Q_SUB * TK
assert TM == TK
assert Q_SUB == 2
BOUNDED_GROUP = 4
CONV_ROWS = 64
VMEM_LIMIT = 56 * 1024 * 1024

F32 = jnp.float32
BF16 = jnp.bfloat16


def _rms(x, g):
    return x * lax.rsqrt(jnp.mean(x * x, axis=-1, keepdims=True) + EPS) * g


def _resident(shape, lead=()):
    index = tuple(lead) + (0,) * len(shape)
    return pl.BlockSpec((None,) * len(lead) + tuple(shape), lambda *_: index, pipeline_mode=pl.Buffered(1))


def _ffn_chunks(f, n_chunks):
    tiles = f // MXU_COLS
    assert tiles * MXU_COLS == f
    edges = [MXU_COLS * (tiles * c // n_chunks) for c in range(n_chunks + 1)]
    return list(zip(edges[:-1], edges[1:]))


def _stage_bf16(src_hbm, which, dst):
    n_rows, n_cols = dst.shape
    rows = max(r for r in range(BF16_SUBLANES, n_rows + 1, BF16_SUBLANES)
               if n_rows % r == 0 and r * n_cols * 4 <= STAGE_CHUNK_BYTES)
    n_chunks = n_rows // rows

    def body(stage, sem):
        def copy(k):
            slot = k % STAGE_SLOTS
            return pltpu.make_async_copy(
                src_hbm.at[which[0], which[1], pl.ds(k * rows, rows), :], stage.at[slot], sem.at[slot])

        for k in range(STAGE_SLOTS - 1):
            copy(k).start()
        for k in range(n_chunks):
            if k + STAGE_SLOTS - 1 < n_chunks:
                copy(k + STAGE_SLOTS - 1).start()
            copy(k).wait()
            dst[k * rows:(k + 1) * rows, :] = stage[k % STAGE_SLOTS].astype(BF16)

    pl.run_scoped(body, pltpu.VMEM((STAGE_SLOTS, rows, n_cols), F32), pltpu.SemaphoreType.DMA((STAGE_SLOTS,)))


def _ffn_kernel(*refs, n_chunks, project, which):
    if project:
        x_ref, a_ref, wp_ref, g_ref, win_hbm, wo_hbm, o_ref, win_ref, wo_ref = refs
    else:
        x_ref, g_ref, win_hbm, wo_hbm, o_ref, win_ref, wo_ref = refs

    @pl.when(pl.program_id(0) == 0)
    def _():
        _stage_bf16(win_hbm, which, win_ref)
        _stage_bf16(wo_hbm, which, wo_ref)

    if project:
        x = x_ref[...] + jnp.dot(a_ref[...], wp_ref[...], preferred_element_type=F32)
    else:
        x = x_ref[...]
    f = wo_ref.shape[0]
    h = _rms(x, g_ref[...]).astype(BF16)
    y = jnp.zeros(x.shape, F32)
    for lo, hi in _ffn_chunks(f, n_chunks):
        gate = jnp.dot(h, win_ref[:, lo:hi], preferred_element_type=F32)
        up = jnp.dot(h, win_ref[:, f + lo:f + hi], preferred_element_type=F32)
        act = (gate * jax.nn.sigmoid(gate) * up).astype(BF16)
        y = y + jnp.dot(act, wo_ref[lo:hi, :], preferred_element_type=F32)
    o_ref[...] = x + 0.5 * y


def _ffn(x2, g, w_in, w_out, which, proj=None):
    t, d = x2.shape
    f = w_out.shape[-2]
    row = pl.BlockSpec((FFN_TM, d), lambda i: (i, 0))
    proj_specs = [] if proj is None else [row, _resident((d, d))]
    proj_args = () if proj is None else proj
    return pl.pallas_call(
        functools.partial(_ffn_kernel, n_chunks=FFN_CHUNKS, project=proj is not None, which=which),
        grid=(t // FFN_TM,),
        in_specs=[row, *proj_specs, _resident((1, d)),
                  pl.BlockSpec(memory_space=pl.ANY), pl.BlockSpec(memory_space=pl.ANY)],
        out_specs=row,
        out_shape=jax.ShapeDtypeStruct((t, d), F32),
        scratch_shapes=[pltpu.VMEM((d, 2 * f), BF16), pltpu.VMEM((f, d), BF16)],
        compiler_params=pltpu.CompilerParams(
            dimension_semantics=("arbitrary",), vmem_limit_bytes=VMEM_LIMIT),
        name="ffn" if proj is None else "proj_ffn",
    )(x2, *proj_args, g, w_in, w_out)


def _conv_kernel(x_ref, g_ref, win_ref, bin_ref, dw_ref, dwb_ref,
                 lng_ref, lnb_ref, wo_ref, bo_ref, o_ref, ubuf, cbuf):
    tm, d = x_ref.shape[1], x_ref.shape[2]
    n_lane_chunks = d // LANES

    @pl.when(pl.program_id(1) == 0)
    def _():
        ubuf[:, 0:HALO, :] = jnp.zeros((n_lane_chunks, HALO, LANES), F32)

    x = x_ref[0]
    h = _rms(x, g_ref[...]).astype(BF16)

    def project(cb):
        lo, hi = cb * MXU_COLS, (cb + 1) * MXU_COLS
        a = jnp.dot(h, win_ref[:, lo:hi], preferred_element_type=F32) + bin_ref[:, lo:hi]
        b = jnp.dot(h, win_ref[:, d + lo:d + hi], preferred_element_type=F32) + bin_ref[:, d + lo:d + hi]
        return a, b

    def conv_lane_chunk(c):
        for r0 in range(0, tm, CONV_ROWS):
            acc = jnp.zeros((CONV_ROWS, LANES), F32)
            for k in range(CONV_WIDTH):
                start = HALO + r0 - (CONV_WIDTH - 1) + k
                acc = acc + dw_ref[c, k:k + 1, :] * ubuf[c, start:start + CONV_ROWS, :]
            cbuf[c, r0:r0 + CONV_ROWS, :] = acc
        ubuf[c, 0:HALO, :] = ubuf[c, tm:tm + HALO, :]

    chunks_per_block = MXU_COLS // LANES
    n_blocks = d // MXU_COLS
    ab = project(0)
    for cb in range(n_blocks):
        a, b = ab
        u = a * jax.nn.sigmoid(b)
        for s in range(chunks_per_block):
            ubuf[cb * chunks_per_block + s, HALO:HALO + tm, :] = u[:, s * LANES:(s + 1) * LANES]
        if cb + 1 < n_blocks:
            ab = project(cb + 1)
        for s in range(chunks_per_block):
            conv_lane_chunk(cb * chunks_per_block + s)

    y = jnp.concatenate([cbuf[c] for c in range(n_lane_chunks)], axis=1) + dwb_ref[...]
    mu = jnp.mean(y, axis=-1, keepdims=True)
    yc = y - mu
    var = jnp.mean(yc * yc, axis=-1, keepdims=True)
    z = yc * lax.rsqrt(var + EPS) * lng_ref[...] + lnb_ref[...]
    z = (z * jax.nn.sigmoid(z)).astype(BF16)
    m = jnp.dot(z, wo_ref[...], preferred_element_type=F32) + bo_ref[...]
    o_ref[0] = x + m


def _conv_mixer(x3, g, w_in, b_in, dw3, dwb, lng, lnb, wo, bo):
    bsz, s, d = x3.shape
    nl = d // LANES
    kpad = dw3.shape[1]
    return pl.pallas_call(
        _conv_kernel,
        grid=(bsz, s // TM),
        in_specs=[
            pl.BlockSpec((1, TM, d), lambda b, i: (b, i, 0)),
            _resident((1, d)),
            _resident((d, 2 * d)), _resident((1, 2 * d)),
            _resident((nl, kpad, LANES)),
            _resident((1, d)), _resident((1, d)), _resident((1, d)),
            _resident((d, d)), _resident((1, d)),
        ],
        out_specs=pl.BlockSpec((1, TM, d), lambda b, i: (b, i, 0)),
        out_shape=jax.ShapeDtypeStruct((bsz, s, d), F32),
        scratch_shapes=[
            pltpu.VMEM((nl, HALO + TM, LANES), F32),
            pltpu.VMEM((nl, TM, LANES), F32),
        ],
        compiler_params=pltpu.CompilerParams(
            dimension_semantics=("arbitrary", "arbitrary"), vmem_limit_bytes=VMEM_LIMIT),
        name="conv_mixer",
    )(x3, g, w_in, b_in, dw3, dwb, lng, lnb, wo, bo)


def _qkv_kernel(x_ref, g_ref, w_ref, gqk_ref, kext_ref, qext_ref,
                q1_ref, q2_ref, k1_ref, k2_ref, v_ref):
    n_heads, tm = q1_ref.shape[1], q1_ref.shape[4]
    d = x_ref.shape[1]
    x = x_ref[...]
    h = _rms(x, g_ref[...]).astype(BF16)
    q = jnp.dot(h, w_ref[:, 0:d], preferred_element_type=F32)
    k = jnp.dot(h, w_ref[:, d:2 * d], preferred_element_type=F32)
    v = jnp.dot(h, w_ref[:, 2 * d:3 * d], preferred_element_type=F32)

    def qk_norm_t(a):
        at = a.T.reshape(d // HEAD_DIM, HEAD_DIM, tm)
        at = at * lax.rsqrt(jnp.mean(at * at, axis=1, keepdims=True) + EPS)
        return at.reshape(d, tm)

    qt = qk_norm_t(q).reshape(n_heads, HEAD_WIDTH, tm)
    row_first = lax.broadcasted_iota(jnp.int32, qt.shape, 1) < HEAD_DIM
    qext = jnp.broadcast_to(qext_ref[...].reshape(n_heads, HEAD_WIDTH, 1), qt.shape)
    q1_ref[0, :, 0] = jnp.where(row_first, qt, qext).astype(BF16)
    q2_ref[0, :, 0] = jnp.where(row_first, qext, qt).astype(BF16)
    kn = qk_norm_t(k).T * gqk_ref[...]
    lane_first = (lax.broadcasted_iota(jnp.int32, kn.shape, 1) % HEAD_WIDTH) < HEAD_DIM
    kext = kext_ref[...]
    k1_ref[...] = jnp.where(lane_first, kn, kext).astype(BF16)
    k2_ref[...] = jnp.where(lane_first, kext, kn).astype(BF16)
    v_ref[0, :, 0, 0:HEAD_WIDTH, :] = v.T.reshape(n_heads, HEAD_WIDTH, tm).astype(BF16)
    ones_row = lax.broadcasted_iota(jnp.int32, (n_heads, V_ROWS - HEAD_WIDTH, tm), 1) == 0
    v_ref[0, :, 0, HEAD_WIDTH:V_ROWS, :] = jnp.where(ones_row, 1.0, 0.0).astype(BF16)


def _qkv(x2, g, w_qkv, gqk, kext, qext, bsz):
    t, d = x2.shape
    n_heads = d // HEAD_WIDTH
    n_tiles = t // bsz // TM
    row = pl.BlockSpec((TM, d), lambda i: (i, 0))
    slab = lambda rows: pl.BlockSpec((1, n_heads, 1, rows, TM), lambda i: (i // n_tiles, 0, i % n_tiles, 0, 0))
    slab_shape = lambda rows: jax.ShapeDtypeStruct((bsz, n_heads, n_tiles, rows, TM), BF16)
    rows_shape = jax.ShapeDtypeStruct((t, d), BF16)
    return pl.pallas_call(
        _qkv_kernel,
        grid=(t // TM,),
        in_specs=[row, _resident((1, d)), _resident((d, 3 * d)),
                  _resident((1, d)), _resident((TM, d)), _resident((d, 1))],
        out_specs=[slab(HEAD_WIDTH), slab(HEAD_WIDTH), row, row, slab(V_ROWS)],
        out_shape=[slab_shape(HEAD_WIDTH), slab_shape(HEAD_WIDTH), rows_shape, rows_shape, slab_shape(V_ROWS)],
        compiler_params=pltpu.CompilerParams(
            dimension_semantics=("arbitrary",), vmem_limit_bytes=VMEM_LIMIT),
        name="qkv_proj",
    )(x2, g, w_qkv, gqk, kext, qext)


def _attn_kernel(slope_ref, bound_ref, q1_ref, q2_ref, k1_ref, k2_ref, v_ref, lq1_ref, lk1_ref, lq2_ref,
                 lk2_ref, subg_ref, o_ref, z_sc, p_sc, m_sc, a_sc, acc_sc, *, lambda_init):
    hd = pl.program_id(1)
    qi = pl.program_id(2)
    slope = slope_ref[hd] * LOG2E
    tile_step = slope * TK
    first_tile = qi * Q_SUB

    qs = tuple(jnp.concatenate([q_ref[0, 0, s] for s in range(Q_SUB)], axis=1)
               for q_ref in (q1_ref, q2_ref))
    ks = (k1_ref, k2_ref)

    def scores(j, c):
        kt = ks[c][0, pl.ds(pl.multiple_of(j * TK, TK), TK), :]
        return jnp.dot(kt, qs[c], preferred_element_type=F32)

    def causal(z, key_offset):
        krow = lax.broadcasted_iota(jnp.int32, (TK, TQ), 0)
        qcol = lax.broadcasted_iota(jnp.int32, (TK, TQ), 1)
        return jnp.where(krow + key_offset <= qcol, z, NEG_BIG)

    acc_sc[...] = jnp.zeros(acc_sc.shape, F32)

    def bounded_tiles(j0, n_tiles, with_diagonal):
        qoff = lax.broadcasted_iota(jnp.int32, (1, TQ), 1).astype(F32)
        ref = bound_ref[0] + slope * qoff
        n_wide = n_tiles - 1 if with_diagonal else n_tiles
        values = lambda j: v_ref[0, 0, j, 0:HEAD_WIDTH, :]
        for c in range(2):
            ps = []
            denom = jnp.zeros((1, TQ), F32)
            for t in range(n_wide):
                z = scores(j0 + t, c)
                if with_diagonal and t == n_wide - 1:
                    z = causal(z, 0)
                off = tile_step * (j0 + t - first_tile).astype(F32) - ref
                p = jnp.exp2(z + off)
                denom = denom + jnp.sum(p, axis=0, keepdims=True)
                ps.append(p.astype(BF16))
            vt = jnp.concatenate([values(j0 + t) for t in range(n_wide)], axis=1)
            acc_sc[c, 0:HEAD_WIDTH, :] += jnp.dot(vt, jnp.concatenate(ps, axis=0), preferred_element_type=F32)
            acc_sc[c, HEAD_WIDTH:HEAD_WIDTH + 1, :] += denom
            if with_diagonal:
                jl = j0 + n_wide
                kt = ks[c][0, pl.ds(pl.multiple_of(jl * TK, TK), TK), :]
                z = jnp.dot(kt, qs[c][:, TK:], preferred_element_type=F32)
                krow = lax.broadcasted_iota(jnp.int32, (TK, TK), 0)
                qcol = lax.broadcasted_iota(jnp.int32, (TK, TK), 1)
                z = jnp.where(krow <= qcol, z, NEG_BIG)
                off = tile_step * (jl - first_tile).astype(F32) - ref[:, TK:]
                p = jnp.exp2(z + off)
                acc_sc[c, HEAD_WIDTH:HEAD_WIDTH + 1, TK:] += jnp.sum(p, axis=0, keepdims=True)
                acc_sc[c, 0:HEAD_WIDTH, TK:] += jnp.dot(values(jl), p.astype(BF16), preferred_element_type=F32)

    def bounded_path():
        def group(t, carry):
            bounded_tiles(BOUNDED_GROUP * t, BOUNDED_GROUP, with_diagonal=False)
            return carry

        lax.fori_loop(0, first_tile // BOUNDED_GROUP, group, 0)
        for rest in range(0, BOUNDED_GROUP, Q_SUB):
            pl.when(first_tile % BOUNDED_GROUP == rest)(
                functools.partial(bounded_tiles, first_tile - rest, rest + Q_SUB, with_diagonal=True))

    def softmax(z, j, c, key_offset):
        if key_offset is not None:
            z = causal(z, key_offset)
        shift = tile_step * (j - first_tile).astype(F32)
        m_old = m_sc[c]
        m_new = jnp.maximum(m_old, jnp.max(z, axis=0, keepdims=True) + shift)
        m_sc[c] = m_new
        return jnp.exp2(z - (m_new - shift)).astype(BF16), jnp.exp2(m_old - m_new)

    def accumulate(c, j, p, alpha):
        acc_sc[c] = alpha * acc_sc[c] + jnp.dot(v_ref[0, 0, j], p, preferred_element_type=F32)

    def step(j, par, key_offset=None, last=False):
        zb = scores(j, 1)
        pa, alpha_a = softmax(z_sc[par], j, 0, key_offset)
        accumulate(1, jnp.maximum(j - 1, 0), p_sc[1 - par], a_sc[...])
        if not last:
            z_sc[1 - par] = scores(j + 1, 0)
        pb, alpha_b = softmax(zb, j, 1, key_offset)
        accumulate(0, j, pa, alpha_a)
        if last:
            accumulate(1, j, pb, alpha_b)
        else:
            p_sc[par] = pb
            a_sc[...] = alpha_b

    def running_max_path():
        m_sc[...] = jnp.full(m_sc.shape, NEG_BIG, F32)
        a_sc[...] = jnp.ones(a_sc.shape, F32)
        p_sc[1] = jnp.zeros(p_sc.shape[1:], BF16)
        z_sc[0] = scores(0, 0)

        def pair(t, carry):
            step(2 * t, 0)
            step(2 * t + 1, 1)
            return carry

        lax.fori_loop(0, first_tile // 2, pair, 0)
        for s in range(Q_SUB):
            step(first_tile + s, s % 2, key_offset=s * TK, last=s == Q_SUB - 1)

    bounded = bound_ref[0] <= MAX_SAFE_BOUND
    pl.when(bounded)(bounded_path)
    pl.when(jnp.logical_not(bounded))(running_max_path)

    lam = (jnp.exp(jnp.sum(lq1_ref[...] * lk1_ref[...], keepdims=True))
           - jnp.exp(jnp.sum(lq2_ref[...] * lk2_ref[...], keepdims=True))
           + lambda_init)
    acc_a, acc_b = acc_sc[0], acc_sc[1]
    ot = (acc_a[:HEAD_WIDTH] / acc_a[HEAD_WIDTH:HEAD_WIDTH + 1]
          - lam * (acc_b[:HEAD_WIDTH] / acc_b[HEAD_WIDTH:HEAD_WIDTH + 1]))
    ot = ot * lax.rsqrt(jnp.mean(ot * ot, axis=0, keepdims=True) + EPS) * subg_ref[...]
    o_ref[0] = (ot * (1.0 - lambda_init)).T.astype(BF16)


def _attention(slopes, bound, q1, q2, k1, k2, v, lq1, lk1, lq2, lk2, subg, lambda_init):
    bsz, n_heads, n_tiles = q1.shape[:3]
    s, d = k1.shape[1], k1.shape[2]
    qspec = pl.BlockSpec((1, 1, Q_SUB, HEAD_WIDTH, TK), lambda b, h, i: (b, h, i, 0, 0))
    kspec = pl.BlockSpec((1, s, HEAD_WIDTH), lambda b, h, i: (b, 0, h))
    vspec = pl.BlockSpec((1, 1, n_tiles, V_ROWS, TK), lambda b, h, i: (b, h, 0, 0, 0))
    small = lambda n: pl.BlockSpec((1, n), lambda b, h, i: (0, 0))
    return pl.pallas_call(
        functools.partial(_attn_kernel, lambda_init=lambda_init),
        grid=(bsz, n_heads, s // TQ),
        in_specs=[
            pl.BlockSpec(memory_space=pltpu.SMEM), pl.BlockSpec(memory_space=pltpu.SMEM),
            qspec, qspec, kspec, kspec, vspec,
            small(HEAD_DIM), small(HEAD_DIM), small(HEAD_DIM), small(HEAD_DIM),
            pl.BlockSpec((HEAD_WIDTH, 1), lambda b, h, i: (0, 0)),
        ],
        out_specs=pl.BlockSpec((1, TQ, HEAD_WIDTH), lambda b, h, i: (b, i, h)),
        out_shape=jax.ShapeDtypeStruct((bsz, s, d), BF16),
        scratch_shapes=[
            pltpu.VMEM((2, TK, TQ), F32),
            pltpu.VMEM((2, TK, TQ), BF16),
            pltpu.VMEM((2, 1, TQ), F32),
            pltpu.VMEM((1, TQ), F32),
            pltpu.VMEM((2, V_ROWS, TQ), F32),
        ],
        compiler_params=pltpu.CompilerParams(
            dimension_semantics=("arbitrary", "arbitrary", "arbitrary"),
            vmem_limit_bytes=VMEM_LIMIT),
        name="diff_attn",
    )(slopes, bound, q1, q2, k1, k2, v, lq1, lk1, lq2, lk2, subg)


def _row(v):
    return v.reshape(1, -1)


def _alibi_tables(n_heads):
    slopes = np.exp2(-8.0 * np.arange(1, n_heads + 1) / n_heads)
    assert np.all(np.log2(slopes) == np.round(np.log2(slopes))), "slopes must be exact in bf16"
    pos = np.arange(TM, dtype=np.float32) * np.float32(LOG2E)
    pieces = []
    for _ in range(N_POS_PIECES):
        piece = pos.astype(BF16).astype(np.float32)
        pieces.append(piece)
        pos = pos - piece
    cols = np.stack(pieces, axis=1)
    half = np.pad(cols, ((0, 0), (0, HEAD_DIM - N_POS_PIECES)))
    kext = np.tile(np.concatenate([half, half], axis=1), (1, n_heads))
    slot = (np.arange(HEAD_WIDTH) % HEAD_DIM) < N_POS_PIECES
    qext = (slopes[:, None] * slot[None, :]).reshape(-1, 1)
    return jnp.asarray(slopes, F32), jnp.asarray(kext, F32), jnp.asarray(qext, F32)


def kernel(x, ffn_norm, ffn_w_in, ffn_w_out, mix_norm, conv_w_in, conv_b_in, conv_dw, conv_dw_b, conv_ln_g, conv_ln_b, conv_w_out, conv_b_out, attn_w_qkv, attn_q_norm, attn_k_norm, attn_lq1, attn_lk1, attn_lq2, attn_lk2, attn_subln_g, attn_w_out):
    bsz, s, d = x.shape
    depth = ffn_norm.shape[0]
    n_heads = d // HEAD_WIDTH
    n_mixers = 2

    def ffn(x2, i, j, proj=None):
        return _ffn(x2, _row(ffn_norm[i, j]), ffn_w_in, ffn_w_out, (i, j), proj)

    x2 = x.reshape(bsz * s, d)
    for i in range(depth):
        x2 = ffn(x2, i, 0)
        j = i // n_mixers
        if i % n_mixers == 0:
            kpad = -(-CONV_WIDTH // 8) * 8
            dw3 = jnp.pad(conv_dw[j], ((0, kpad - CONV_WIDTH), (0, 0)))
            dw3 = dw3.reshape(kpad, d // LANES, LANES).transpose(1, 0, 2)
            x2 = _conv_mixer(
                x2.reshape(bsz, s, d), _row(mix_norm[i]), conv_w_in[j].astype(BF16), _row(conv_b_in[j]),
                dw3, _row(conv_dw_b[j]), _row(conv_ln_g[j]), _row(conv_ln_b[j]),
                conv_w_out[j].astype(BF16), _row(conv_b_out[j])).reshape(bsz * s, d)
            x2 = ffn(x2, i, 1)
        else:
            lambda_init = 0.8 - 0.6 * math.exp(-0.3 * i)
            gqk = _row(jnp.tile(attn_q_norm[j] * attn_k_norm[j], d // HEAD_DIM)) * (LOG2E / math.sqrt(HEAD_DIM))
            slopes, kext, qext = _alibi_tables(n_heads)
            q1, q2, k1, k2, v = _qkv(x2, _row(mix_norm[i]), attn_w_qkv[j].astype(BF16),
                                     gqk, kext, qext, bsz)
            bound = (HEAD_DIM * BOUND_MARGIN * jnp.max(jnp.abs(gqk))).reshape(1)
            o = _attention(slopes, bound, q1, q2, k1.reshape(bsz, s, d), k2.reshape(bsz, s, d), v,
                           _row(attn_lq1[j]), _row(attn_lk1[j]), _row(attn_lq2[j]), _row(attn_lk2[j]),
                           attn_subln_g[j].reshape(-1, 1), lambda_init)
            x2 = ffn(x2, i, 1, proj=(o.reshape(bsz * s, d), attn_w_out[j].astype(BF16)))
    return x2.reshape(bsz, s, d)
```

```python
import functools
import math

import jax
import jax.numpy as jnp
import numpy as np
from jax import lax
from jax.experimental import pallas as pl
from jax.experimental.pallas import tpu as pltpu

EPS = 1e-6
CONV_WIDTH = 31
HEAD_DIM = 64
HEAD_WIDTH = 2 * HEAD_DIM
LANES = 128
MXU_COLS = 256
BF16_SUBLANES = 16
V_ROWS = HEAD_WIDTH + BF16_SUBLANES
N_POS_PIECES = 3
HALO = 32
NEG_BIG = -1e30
LOG2E = 1.4426950408889634
BOUND_MARGIN = 1.02
MAX_SAFE_BOUND = 40.0

TM = 512
FFN_TM = 1024
FFN_CHUNKS = 4
STAGE_CHUNK_BYTES = 768 * 1024
STAGE_SLOTS = 5
TK = 512
Q_SUB = 2
TQ = Q_SUB * TK
assert TM == TK
assert Q_SUB == 2
BOUNDED_GROUP = 4
CONV_ROWS = 64
VMEM_LIMIT = 56 * 1024 * 1024

F32 = jnp.float32
BF16 = jnp.bfloat16


def _rms(x, g):
    return x * lax.rsqrt(jnp.mean(x * x, axis=-1, keepdims=True) + EPS) * g


def _resident(shape, lead=()):
    index = tuple(lead) + (0,) * len(shape)
    return pl.BlockSpec((None,) * len(lead) + tuple(shape), lambda *_: index, pipeline_mode=pl.Buffered(1))


def _ffn_chunks(f, n_chunks):
    tiles = f // MXU_COLS
    assert tiles * MXU_COLS == f
    edges = [MXU_COLS * (tiles * c // n_chunks) for c in range(n_chunks + 1)]
    return list(zip(edges[:-1], edges[1:]))


def _stage_bf16(src_hbm, which, dst):
    n_rows, n_cols = dst.shape
    rows = max(r for r in range(BF16_SUBLANES, n_rows + 1, BF16_SUBLANES)
               if n_rows % r == 0 and r * n_cols * 4 <= STAGE_CHUNK_BYTES)
    n_chunks = n_rows // rows

    def body(stage, sem):
        def copy(k):
            slot = k % STAGE_SLOTS
            return pltpu.make_async_copy(
                src_hbm.at[which[0], which[1], pl.ds(k * rows, rows), :], stage.at[slot], sem.at[slot])

        for k in range(STAGE_SLOTS - 1):
            copy(k).start(priority=k % 2)
        for k in range(n_chunks):
            if k + STAGE_SLOTS - 1 < n_chunks:
                ahead = k + STAGE_SLOTS - 1
                copy(ahead).start(priority=ahead % 2)
            copy(k).wait()
            dst[k * rows:(k + 1) * rows, :] = stage[k % STAGE_SLOTS].astype(BF16)

    pl.run_scoped(body, pltpu.VMEM((STAGE_SLOTS, rows, n_cols), F32), pltpu.SemaphoreType.DMA((STAGE_SLOTS,)))


def _ffn_kernel(*refs, n_chunks, project, which):
    if project:
        x_ref, a_ref, wp_ref, g_ref, win_hbm, wo_hbm, o_ref, win_ref, wo_ref = refs
    else:
        x_ref, g_ref, win_hbm, wo_hbm, o_ref, win_ref, wo_ref = refs

    @pl.when(pl.program_id(0) == 0)
    def _():
        _stage_bf16(win_hbm, which, win_ref)
        _stage_bf16(wo_hbm, which, wo_ref)

    if project:
        x = x_ref[...] + jnp.dot(a_ref[...], wp_ref[...], preferred_element_type=F32)
    else:
        x = x_ref[...]
    f = wo_ref.shape[0]
    h = _rms(x, g_ref[...]).astype(BF16)
    y = jnp.zeros(x.shape, F32)
    for lo, hi in _ffn_chunks(f, n_chunks):
        gate = jnp.dot(h, win_ref[:, lo:hi], preferred_element_type=F32)
        up = jnp.dot(h, win_ref[:, f + lo:f + hi], preferred_element_type=F32)
        act = (gate * jax.nn.sigmoid(gate) * up).astype(BF16)
        y = y + jnp.dot(act, wo_ref[lo:hi, :], preferred_element_type=F32)
    o_ref[...] = x + 0.5 * y


def _ffn(x2, g, w_in, w_out, which, proj=None):
    t, d = x2.shape
    f = w_out.shape[-2]
    row = pl.BlockSpec((FFN_TM, d), lambda i: (i, 0))
    proj_specs = [] if proj is None else [row, _resident((d, d))]
    proj_args = () if proj is None else proj
    return pl.pallas_call(
        functools.partial(_ffn_kernel, n_chunks=FFN_CHUNKS, project=proj is not None, which=which),
        grid=(t // FFN_TM,),
        in_specs=[row, *proj_specs, _resident((1, d)),
                  pl.BlockSpec(memory_space=pl.ANY), pl.BlockSpec(memory_space=pl.ANY)],
        out_specs=row,
        out_shape=jax.ShapeDtypeStruct((t, d), F32),
        scratch_shapes=[pltpu.VMEM((d, 2 * f), BF16), pltpu.VMEM((f, d), BF16)],
        compiler_params=pltpu.CompilerParams(
            dimension_semantics=("arbitrary",), vmem_limit_bytes=VMEM_LIMIT),
        name="ffn" if proj is None else "proj_ffn",
    )(x2, *proj_args, g, w_in, w_out)


def _conv_kernel(x_ref, g_ref, win_ref, bin_ref, dw_ref, dwb_ref,
                 lng_ref, lnb_ref, wo_ref, bo_ref, o_ref, ubuf, cbuf):
    tm, d = x_ref.shape[1], x_ref.shape[2]
    n_lane_chunks = d // LANES

    @pl.when(pl.program_id(1) == 0)
    def _():
        ubuf[:, 0:HALO, :] = jnp.zeros((n_lane_chunks, HALO, LANES), F32)

    x = x_ref[0]
    h = _rms(x, g_ref[...]).astype(BF16)

    def project(cb):
        lo, hi = cb * MXU_COLS, (cb + 1) * MXU_COLS
        a = jnp.dot(h, win_ref[:, lo:hi], preferred_element_type=F32) + bin_ref[:, lo:hi]
        b = jnp.dot(h, win_ref[:, d + lo:d + hi], preferred_element_type=F32) + bin_ref[:, d + lo:d + hi]
        return a, b

    def conv_lane_chunk(c):
        for r0 in range(0, tm, CONV_ROWS):
            acc = jnp.zeros((CONV_ROWS, LANES), F32)
            for k in range(CONV_WIDTH):
                start = HALO + r0 - (CONV_WIDTH - 1) + k
                acc = acc + dw_ref[c, k:k + 1, :] * ubuf[c, start:start + CONV_ROWS, :]
            cbuf[c, r0:r0 + CONV_ROWS, :] = acc
        ubuf[c, 0:HALO, :] = ubuf[c, tm:tm + HALO, :]

    chunks_per_block = MXU_COLS // LANES
    n_blocks = d // MXU_COLS
    ab = project(0)
    for cb in range(n_blocks):
        a, b = ab
        u = a * jax.nn.sigmoid(b)
        for s in range(chunks_per_block):
            ubuf[cb * chunks_per_block + s, HALO:HALO + tm, :] = u[:, s * LANES:(s + 1) * LANES]
        if cb + 1 < n_blocks:
            ab = project(cb + 1)
        for s in range(chunks_per_block):
            conv_lane_chunk(cb * chunks_per_block + s)

    y = jnp.concatenate([cbuf[c] for c in range(n_lane_chunks)], axis=1) + dwb_ref[...]
    mu = jnp.mean(y, axis=-1, keepdims=True)
    yc = y - mu
    var = jnp.mean(yc * yc, axis=-1, keepdims=True)
    z = yc * lax.rsqrt(var + EPS) * lng_ref[...] + lnb_ref[...]
    z = (z * jax.nn.sigmoid(z)).astype(BF16)
    m = jnp.dot(z, wo_ref[...], preferred_element_type=F32) + bo_ref[...]
    o_ref[0] = x + m


def _conv_mixer(x3, g, w_in, b_in, dw3, dwb, lng, lnb, wo, bo):
    bsz, s, d = x3.shape
    nl = d // LANES
    kpad = dw3.shape[1]
    return pl.pallas_call(
        _conv_kernel,
        grid=(bsz, s // TM),
        in_specs=[
            pl.BlockSpec((1, TM, d), lambda b, i: (b, i, 0)),
            _resident((1, d)),
            _resident((d, 2 * d)), _resident((1, 2 * d)),
            _resident((nl, kpad, LANES)),
            _resident((1, d)), _resident((1, d)), _resident((1, d)),
            _resident((d, d)), _resident((1, d)),
        ],
        out_specs=pl.BlockSpec((1, TM, d), lambda b, i: (b, i, 0)),
        out_shape=jax.ShapeDtypeStruct((bsz, s, d), F32),
        scratch_shapes=[
            pltpu.VMEM((nl, HALO + TM, LANES), F32),
            pltpu.VMEM((nl, TM, LANES), F32),
        ],
        compiler_params=pltpu.CompilerParams(
            dimension_semantics=("arbitrary", "arbitrary"), vmem_limit_bytes=VMEM_LIMIT),
        name="conv_mixer",
    )(x3, g, w_in, b_in, dw3, dwb, lng, lnb, wo, bo)


def _qkv_kernel(x_ref, g_ref, w_ref, gqk_ref, kext_ref, qext_ref,
                q1_ref, q2_ref, k1_ref, k2_ref, v_ref):
    n_heads, tm = q1_ref.shape[1], q1_ref.shape[4]
    d = x_ref.shape[1]
    x = x_ref[...]
    h = _rms(x, g_ref[...]).astype(BF16)
    q = jnp.dot(h, w_ref[:, 0:d], preferred_element_type=F32)
    k = jnp.dot(h, w_ref[:, d:2 * d], preferred_element_type=F32)
    v = jnp.dot(h, w_ref[:, 2 * d:3 * d], preferred_element_type=F32)

    def qk_norm_t(a):
        at = a.T.reshape(d // HEAD_DIM, HEAD_DIM, tm)
        at = at * lax.rsqrt(jnp.mean(at * at, axis=1, keepdims=True) + EPS)
        return at.reshape(d, tm)

    qt = qk_norm_t(q).reshape(n_heads, HEAD_WIDTH, tm)
    row_first = lax.broadcasted_iota(jnp.int32, qt.shape, 1) < HEAD_DIM
    qext = jnp.broadcast_to(qext_ref[...].reshape(n_heads, HEAD_WIDTH, 1), qt.shape)
    q1_ref[0, :, 0] = jnp.where(row_first, qt, qext).astype(BF16)
    q2_ref[0, :, 0] = jnp.where(row_first, qext, qt).astype(BF16)
    kn = qk_norm_t(k).T * gqk_ref[...]
    lane_first = (lax.broadcasted_iota(jnp.int32, kn.shape, 1) % HEAD_WIDTH) < HEAD_DIM
    kext = kext_ref[...]
    k1_ref[...] = jnp.where(lane_first, kn, kext).astype(BF16)
    k2_ref[...] = jnp.where(lane_first, kext, kn).astype(BF16)
    v_ref[0, :, 0, 0:HEAD_WIDTH, :] = v.T.reshape(n_heads, HEAD_WIDTH, tm).astype(BF16)
    ones_row = lax.broadcasted_iota(jnp.int32, (n_heads, V_ROWS - HEAD_WIDTH, tm), 1) == 0
    v_ref[0, :, 0, HEAD_WIDTH:V_ROWS, :] = jnp.where(ones_row, 1.0, 0.0).astype(BF16)


def _qkv(x2, g, w_qkv, gqk, kext, qext, bsz):
    t, d = x2.shape
    n_heads = d // HEAD_WIDTH
    n_tiles = t // bsz // TM
    row = pl.BlockSpec((TM, d), lambda i: (i, 0))
    slab = lambda rows: pl.BlockSpec((1, n_heads, 1, rows, TM), lambda i: (i // n_tiles, 0, i % n_tiles, 0, 0))
    slab_shape = lambda rows: jax.ShapeDtypeStruct((bsz, n_heads, n_tiles, rows, TM), BF16)
    rows_shape = jax.ShapeDtypeStruct((t, d), BF16)
    return pl.pallas_call(
        _qkv_kernel,
        grid=(t // TM,),
        in_specs=[row, _resident((1, d)), _resident((d, 3 * d)),
                  _resident((1, d)), _resident((TM, d)), _resident((d, 1))],
        out_specs=[slab(HEAD_WIDTH), slab(HEAD_WIDTH), row, row, slab(V_ROWS)],
        out_shape=[slab_shape(HEAD_WIDTH), slab_shape(HEAD_WIDTH), rows_shape, rows_shape, slab_shape(V_ROWS)],
        compiler_params=pltpu.CompilerParams(
            dimension_semantics=("arbitrary",), vmem_limit_bytes=VMEM_LIMIT),
        name="qkv_proj",
    )(x2, g, w_qkv, gqk, kext, qext)


def _attn_kernel(slope_ref, bound_ref, q1_ref, q2_ref, k1_ref, k2_ref, v_ref, lq1_ref, lk1_ref, lq2_ref,
                 lk2_ref, subg_ref, o_ref, z_sc, p_sc, m_sc, a_sc, acc_sc, *, lambda_init):
    hd = pl.program_id(1)
    qi = pl.program_id(2)
    slope = slope_ref[hd] * LOG2E
    tile_step = slope * TK
    first_tile = qi * Q_SUB

    qs = tuple(jnp.concatenate([q_ref[0, 0, s] for s in range(Q_SUB)], axis=1)
               for q_ref in (q1_ref, q2_ref))
    ks = (k1_ref, k2_ref)

    def scores(j, c):
        kt = ks[c][0, pl.ds(pl.multiple_of(j * TK, TK), TK), :]
        return jnp.dot(kt, qs[c], preferred_element_type=F32)

    def causal(z, key_offset):
        krow = lax.broadcasted_iota(jnp.int32, (TK, TQ), 0)
        qcol = lax.broadcasted_iota(jnp.int32, (TK, TQ), 1)
        return jnp.where(krow + key_offset <= qcol, z, NEG_BIG)

    acc_sc[...] = jnp.zeros(acc_sc.shape, F32)

    def bounded_tiles(j0, n_tiles, with_diagonal):
        qoff = lax.broadcasted_iota(jnp.int32, (1, TQ), 1).astype(F32)
        ref = bound_ref[0] + slope * qoff
        n_wide = n_tiles - 1 if with_diagonal else n_tiles
        values = lambda j: v_ref[0, 0, j, 0:HEAD_WIDTH, :]
        for c in range(2):
            ps = []
            denom = jnp.zeros((1, TQ), F32)
            for t in range(n_wide):
                z = scores(j0 + t, c)
                if with_diagonal and t == n_wide - 1:
                    z = causal(z, 0)
                off = tile_step * (j0 + t - first_tile).astype(F32) - ref
                p = jnp.exp2(z + off)
                denom = denom + jnp.sum(p, axis=0, keepdims=True)
                ps.append(p.astype(BF16))
            vt = jnp.concatenate([values(j0 + t) for t in range(n_wide)], axis=1)
            acc_sc[c, 0:HEAD_WIDTH, :] += jnp.dot(vt, jnp.concatenate(ps, axis=0), preferred_element_type=F32)
            acc_sc[c, HEAD_WIDTH:HEAD_WIDTH + 1, :] += denom
            if with_diagonal:
                jl = j0 + n_wide
                kt = ks[c][0, pl.ds(pl.multiple_of(jl * TK, TK), TK), :]
                z = jnp.dot(kt, qs[c][:, TK:], preferred_element_type=F32)
                krow = lax.broadcasted_iota(jnp.int32, (TK, TK), 0)
                qcol = lax.broadcasted_iota(jnp.int32, (TK, TK), 1)
                z = jnp.where(krow <= qcol, z, NEG_BIG)
                off = tile_step * (jl - first_tile).astype(F32) - ref[:, TK:]
                p = jnp.exp2(z + off)
                acc_sc[c, HEAD_WIDTH:HEAD_WIDTH + 1, TK:] += jnp.sum(p, axis=0, keepdims=True)
                acc_sc[c, 0:HEAD_WIDTH, TK:] += jnp.dot(values(jl), p.astype(BF16), preferred_element_type=F32)

    def bounded_path():
        def group(t, carry):
            bounded_tiles(BOUNDED_GROUP * t, BOUNDED_GROUP, with_diagonal=False)
            return carry

        lax.fori_loop(0, first_tile // BOUNDED_GROUP, group, 0)
        for rest in range(0, BOUNDED_GROUP, Q_SUB):
            pl.when(first_tile % BOUNDED_GROUP == rest)(
                functools.partial(bounded_tiles, first_tile - rest, rest + Q_SUB, with_diagonal=True))

    def softmax(z, j, c, key_offset):
        if key_offset is not None:
            z = causal(z, key_offset)
        shift = tile_step * (j - first_tile).astype(F32)
        m_old = m_sc[c]
        m_new = jnp.maximum(m_old, jnp.max(z, axis=0, keepdims=True) + shift)
        m_sc[c] = m_new
        return jnp.exp2(z - (m_new - shift)).astype(BF16), jnp.exp2(m_old - m_new)

    def accumulate(c, j, p, alpha):
        acc_sc[c] = alpha * acc_sc[c] + jnp.dot(v_ref[0, 0, j], p, preferred_element_type=F32)

    def step(j, par, key_offset=None, last=False):
        zb = scores(j, 1)
        pa, alpha_a = softmax(z_sc[par], j, 0, key_offset)
        accumulate(1, jnp.maximum(j - 1, 0), p_sc[1 - par], a_sc[...])
        if not last:
            z_sc[1 - par] = scores(j + 1, 0)
        pb, alpha_b = softmax(zb, j, 1, key_offset)
        accumulate(0, j, pa, alpha_a)
        if last:
            accumulate(1, j, pb, alpha_b)
        else:
            p_sc[par] = pb
            a_sc[...] = alpha_b

    def running_max_path():
        m_sc[...] = jnp.full(m_sc.shape, NEG_BIG, F32)
        a_sc[...] = jnp.ones(a_sc.shape, F32)
        p_sc[1] = jnp.zeros(p_sc.shape[1:], BF16)
        z_sc[0] = scores(0, 0)

        def pair(t, carry):
            step(2 * t, 0)
            step(2 * t + 1, 1)
            return carry

        lax.fori_loop(0, first_tile // 2, pair, 0)
        for s in range(Q_SUB):
            step(first_tile + s, s % 2, key_offset=s * TK, last=s == Q_SUB - 1)

    bounded = bound_ref[0] <= MAX_SAFE_BOUND
    pl.when(bounded)(bounded_path)
    pl.when(jnp.logical_not(bounded))(running_max_path)

    lam = (jnp.exp(jnp.sum(lq1_ref[...] * lk1_ref[...], keepdims=True))
           - jnp.exp(jnp.sum(lq2_ref[...] * lk2_ref[...], keepdims=True))
           + lambda_init)
    acc_a, acc_b = acc_sc[0], acc_sc[1]
    ot = (acc_a[:HEAD_WIDTH] / acc_a[HEAD_WIDTH:HEAD_WIDTH + 1]
          - lam * (acc_b[:HEAD_WIDTH] / acc_b[HEAD_WIDTH:HEAD_WIDTH + 1]))
    ot = ot * lax.rsqrt(jnp.mean(ot * ot, axis=0, keepdims=True) + EPS) * subg_ref[...]
    o_ref[0] = (ot * (1.0 - lambda_init)).T.astype(BF16)


def _attention(slopes, bound, q1, q2, k1, k2, v, lq1, lk1, lq2, lk2, subg, lambda_init):
    bsz, n_heads, n_tiles = q1.shape[:3]
    s, d = k1.shape[1], k1.shape[2]
    qspec = pl.BlockSpec((1, 1, Q_SUB, HEAD_WIDTH, TK), lambda b, h, i: (b, h, i, 0, 0))
    kspec = pl.BlockSpec((1, s, HEAD_WIDTH), lambda b, h, i: (b, 0, h))
    vspec = pl.BlockSpec((1, 1, n_tiles, V_ROWS, TK), lambda b, h, i: (b, h, 0, 0, 0))
    small = lambda n: pl.BlockSpec((1, n), lambda b, h, i: (0, 0))
    return pl.pallas_call(
        functools.partial(_attn_kernel, lambda_init=lambda_init),
        grid=(bsz, n_heads, s // TQ),
        in_specs=[
            pl.BlockSpec(memory_space=pltpu.SMEM), pl.BlockSpec(memory_space=pltpu.SMEM),
            qspec, qspec, kspec, kspec, vspec,
            small(HEAD_DIM), small(HEAD_DIM), small(HEAD_DIM), small(HEAD_DIM),
            pl.BlockSpec((HEAD_WIDTH, 1), lambda b, h, i: (0, 0)),
        ],
        out_specs=pl.BlockSpec((1, TQ, HEAD_WIDTH), lambda b, h, i: (b, i, h)),
        out_shape=jax.ShapeDtypeStruct((bsz, s, d), BF16),
        scratch_shapes=[
            pltpu.VMEM((2, TK, TQ), F32),
            pltpu.VMEM((2, TK, TQ), BF16),
            pltpu.VMEM((2, 1, TQ), F32),
            pltpu.VMEM((1, TQ), F32),
            pltpu.VMEM((2, V_ROWS, TQ), F32),
        ],
        compiler_params=pltpu.CompilerParams(
            dimension_semantics=("arbitrary", "arbitrary", "arbitrary"),
            vmem_limit_bytes=VMEM_LIMIT),
        name="diff_attn",
    )(slopes, bound, q1, q2, k1, k2, v, lq1, lk1, lq2, lk2, subg)


def _row(v):
    return v.reshape(1, -1)


def _alibi_tables(n_heads):
    slopes = np.exp2(-8.0 * np.arange(1, n_heads + 1) / n_heads)
    assert np.all(np.log2(slopes) == np.round(np.log2(slopes))), "slopes must be exact in bf16"
    pos = np.arange(TM, dtype=np.float32) * np.float32(LOG2E)
    pieces = []
    for _ in range(N_POS_PIECES):
        piece = pos.astype(BF16).astype(np.float32)
        pieces.append(piece)
        pos = pos - piece
    cols = np.stack(pieces, axis=1)
    half = np.pad(cols, ((0, 0), (0, HEAD_DIM - N_POS_PIECES)))
    kext = np.tile(np.concatenate([half, half], axis=1), (1, n_heads))
    slot = (np.arange(HEAD_WIDTH) % HEAD_DIM) < N_POS_PIECES
    qext = (slopes[:, None] * slot[None, :]).reshape(-1, 1)
    return jnp.asarray(slopes, F32), jnp.asarray(kext, F32), jnp.asarray(qext, F32)


def kernel(x, ffn_norm, ffn_w_in, ffn_w_out, mix_norm, conv_w_in, conv_b_in, conv_dw, conv_dw_b, conv_ln_g, conv_ln_b, conv_w_out, conv_b_out, attn_w_qkv, attn_q_norm, attn_k_norm, attn_lq1, attn_lk1, attn_lq2, attn_lk2, attn_subln_g, attn_w_out):
    bsz, s, d = x.shape
    depth = ffn_norm.shape[0]
    n_heads = d // HEAD_WIDTH
    n_mixers = 2

    def ffn(x2, i, j, proj=None):
        return _ffn(x2, _row(ffn_norm[i, j]), ffn_w_in, ffn_w_out, (i, j), proj)

    x2 = x.reshape(bsz * s, d)
    for i in range(depth):
        x2 = ffn(x2, i, 0)
        j = i // n_mixers
        if i % n_mixers == 0:
            kpad = -(-CONV_WIDTH // 8) * 8
            dw3 = jnp.pad(conv_dw[j], ((0, kpad - CONV_WIDTH), (0, 0)))
            dw3 = dw3.reshape(kpad, d // LANES, LANES).transpose(1, 0, 2)
            x2 = _conv_mixer(
                x2.reshape(bsz, s, d), _row(mix_norm[i]), conv_w_in[j].astype(BF16), _row(conv_b_in[j]),
                dw3, _row(conv_dw_b[j]), _row(conv_ln_g[j]), _row(conv_ln_b[j]),
                conv_w_out[j].astype(BF16), _row(conv_b_out[j])).reshape(bsz * s, d)
            x2 = ffn(x2, i, 1)
        else:
            lambda_init = 0.8 - 0.6 * math.exp(-0.3 * i)
            gqk = _row(jnp.tile(attn_q_norm[j] * attn_k_norm[j], d // HEAD_DIM)) * (LOG2E / math.sqrt(HEAD_DIM))
            slopes, kext, qext = _alibi_tables(n_heads)
            q1, q2, k1, k2, v = _qkv(x2, _row(mix_norm[i]), attn_w_qkv[j].astype(BF16),
                                     gqk, kext, qext, bsz)
            bound = (HEAD_DIM * BOUND_MARGIN * jnp.max(jnp.abs(gqk))).reshape(1)
            o = _attention(slopes, bound, q1, q2, k1.reshape(bsz, s, d), k2.reshape(bsz, s, d), v,
                           _row(attn_lq1[j]), _row(attn_lk1[j]), _row(attn_lq2[j]), _row(attn_lk2[j]),
                           attn_subln_g[j].reshape(-1, 1), lambda_init)
            x2 = ffn(x2, i, 1, proj=(o.reshape(bsz * s, d), attn_w_out[j].astype(BF16)))
    return x2.reshape(bsz, s, d)
```
